```python
import jax, jax.numpy as jnp
from jax import lax
import numpy as np

D_MODEL = 2048
BATCH = 4
SEQ = 4096
DEPTH = 2

CTX_LEN = 256
GRID_W = 64

D_MIX = D_MODEL
POOL_WIDTH = D_MIX // 4
POOL_WINDOWS = (2, 4, 8, 16)
POOL_GROUPS = 4
FNET_WIDTH = D_MIX // 4
FNET_HEADS = 4
MLA_HEADS = 8
QK_NOPE_DIM = 128
QK_ROPE_DIM = 64
QK_HEAD_DIM = QK_NOPE_DIM + QK_ROPE_DIM
V_HEAD_DIM = 128
MLA_WIDTH = MLA_HEADS * V_HEAD_DIM
Q_LORA_RANK = D_MODEL // 4
KV_LORA_RANK = D_MODEL // 4
N_FREQ_PER_AXIS = QK_ROPE_DIM // 4
ROPE_THETA = 10000.0
EPS = 1e-6
Q_BLOCK = 128
SM_SCALE = QK_HEAD_DIM ** -0.5

OFF_POOL = 0
OFF_POOL_GATE = OFF_POOL + POOL_WIDTH
OFF_FNET = OFF_POOL_GATE + POOL_WIDTH
OFF_FNET_GATE = OFF_FNET + FNET_WIDTH
OFF_CQ = OFF_FNET_GATE + FNET_WIDTH
OFF_CKV = OFF_CQ + Q_LORA_RANK
OFF_KROPE = OFF_CKV + KV_LORA_RANK
OFF_MLA_GATE = OFF_KROPE + QK_ROPE_DIM
D_IN = OFF_MLA_GATE + MLA_WIDTH

kernel_name = 'hybrid_pool_fnet_mla_diffusion_block'


def _rmsnorm(x, w):
    xf = x.astype(jnp.float32)
    y = xf * lax.rsqrt(jnp.mean(xf * xf, axis=-1, keepdims=True) + EPS)
    return (y * w.astype(jnp.float32)).astype(x.dtype)


def _rotate_half(x):
    x1, x2 = jnp.split(x, 2, axis=-1)
    return jnp.concatenate([-x2, x1], axis=-1)


def _axial_rope_tables(n_tokens):
    rows = n_tokens // GRID_W
    r, col = jnp.meshgrid(jnp.arange(rows, dtype=jnp.float32),
                          jnp.arange(GRID_W, dtype=jnp.float32), indexing='ij')
    inv_freq = ROPE_THETA ** (-jnp.arange(N_FREQ_PER_AXIS, dtype=jnp.float32) / N_FREQ_PER_AXIS)
    ang_r = r.reshape(-1)[:, None] * inv_freq
    ang_c = col.reshape(-1)[:, None] * inv_freq
    ang = jnp.concatenate([ang_r, ang_r, ang_c, ang_c], axis=-1)
    return jnp.cos(ang), jnp.sin(ang)


def _apply_axial_rope(x, cos, sin):
    half = QK_ROPE_DIM // 2
    xf = x.astype(jnp.float32)
    rot = jnp.concatenate([_rotate_half(xf[..., :half]), _rotate_half(xf[..., half:])], axis=-1)
    return (xf * cos[:, None, :] + rot * sin[:, None, :]).astype(x.dtype)


def _rope_tail(x, cos, sin):
    return jnp.concatenate([x[..., :QK_NOPE_DIM], _apply_axial_rope(x[..., QK_NOPE_DIM:], cos, sin)], axis=-1)


def _window_mean_minus_self(x, w):
    n = x.shape[1]
    lo = w // 2
    hi = w - lo - 1
    xf = x.astype(jnp.float32)
    csum = jnp.concatenate([jnp.zeros_like(xf[:, :1]), jnp.cumsum(xf, axis=1)], axis=1)
    t = jnp.arange(n)
    start = jnp.clip(t - lo, 0, n)
    end = jnp.clip(t + hi + 1, 0, n)
    count = (end - start).astype(jnp.float32)[None, :, None]
    return ((csum[:, end] - csum[:, start]) / count - xf).astype(x.dtype)


def _pool_mixer(u, pool_w, pool_scale):
    b, n, _ = u.shape
    g = u.reshape(b, n, POOL_GROUPS, POOL_WIDTH // POOL_GROUPS)
    pooled = jnp.stack([_window_mean_minus_self(g[:, :, i], w) for i, w in enumerate(POOL_WINDOWS)], axis=2)
    y = jnp.einsum('blgc,gcd->blgd', pooled, pool_w).reshape(b, n, POOL_WIDTH)
    return y * pool_scale


def _fourier_mixer(u, fnet_w):
    b, n, _ = u.shape
    g = u.reshape(b, n, FNET_HEADS, FNET_WIDTH // FNET_HEADS).astype(jnp.float32)
    f = jnp.fft.fft2(g, axes=(1, 3), norm='ortho').real.astype(u.dtype)
    return jnp.einsum('blgc,gcd->blgd', f, fnet_w).reshape(b, n, FNET_WIDTH)


def _mla_q(p, q_norm_w, w_uq, q_head_norm_w):
    cq = _rmsnorm(p[..., OFF_CQ:OFF_CKV], q_norm_w)
    q = cq @ w_uq
    q = q.reshape(q.shape[0], q.shape[1], MLA_HEADS, QK_HEAD_DIM)
    return _rmsnorm(q, q_head_norm_w)


def _mla_kv(kv_in, kv_norm_w, w_ukv, k_head_norm_w):
    b, n, _ = kv_in.shape
    ckv = _rmsnorm(kv_in[..., :KV_LORA_RANK], kv_norm_w)
    k_rope = kv_in[..., KV_LORA_RANK:]
    kv = (ckv @ w_ukv).reshape(b, n, MLA_HEADS, QK_NOPE_DIM + V_HEAD_DIM)
    k_nope, v = kv[..., :QK_NOPE_DIM], kv[..., QK_NOPE_DIM:]
    k = jnp.concatenate([k_nope, jnp.broadcast_to(k_rope[:, :, None, :], (b, n, MLA_HEADS, QK_ROPE_DIM))], axis=-1)
    return _rmsnorm(k, k_head_norm_w), v


def _softmax_attend(q, k, v):
    s = jnp.einsum('bqhd,bkhd->bhqk', q, k).astype(jnp.float32) * SM_SCALE
    p = jax.nn.softmax(s, axis=-1).astype(v.dtype)
    return jnp.einsum('bhqk,bkhd->bqhd', p, v)


def _attend_blocked(q, k, v):
    b, n, h, dq = q.shape
    nb = n // Q_BLOCK
    qb = q.reshape(b, nb, Q_BLOCK, h, dq).transpose(1, 0, 2, 3, 4)
    o = lax.map(lambda blk: _softmax_attend(blk, k, v), qb)
    return o.transpose(1, 0, 2, 3, 4).reshape(b, n, h * V_HEAD_DIM)


def _mixer_concat(p, mla_out, pool_w, pool_scale, fnet_w):
    a = jax.nn.silu(p[..., OFF_POOL_GATE:OFF_FNET]) * _pool_mixer(p[..., OFF_POOL:OFF_POOL_GATE], pool_w, pool_scale)
    f = jax.nn.silu(p[..., OFF_FNET_GATE:OFF_CQ]) * _fourier_mixer(p[..., OFF_FNET:OFF_FNET_GATE], fnet_w)
    m = jax.nn.silu(p[..., OFF_MLA_GATE:D_IN]) * mla_out
    return jnp.concatenate([a, f, m], axis=-1)


def _layer(x, ctx, c, c_ctx, norm_w, w_ada, b_ada, w_in, pool_w, pool_scale, fnet_w,
           q_norm_w, w_uq, kv_norm_w, w_ukv, q_head_norm_w, k_head_norm_w, w_out, ctx_out):
    shift, scale, gate = jnp.split(jax.nn.silu(c) @ w_ada + b_ada, 3, axis=-1)
    shift_c, scale_c, gate_c = jnp.split(jax.nn.silu(c_ctx) @ w_ada + b_ada, 3, axis=-1)
    h = _rmsnorm(x, norm_w) * (1.0 + scale[:, None, :]) + shift[:, None, :]
    hc = _rmsnorm(ctx, norm_w) * (1.0 + scale_c) + shift_c

    if ctx_out:
        pc = hc @ w_in
        kvc_in = pc[..., OFF_CKV:OFF_MLA_GATE]
    else:
        kvc_in = hc @ w_in[:, OFF_CKV:OFF_MLA_GATE]
    k_c, v_c = _mla_kv(kvc_in, kv_norm_w, w_ukv, k_head_norm_w)

    p = h @ w_in
    cos, sin = _axial_rope_tables(x.shape[1])
    q = _rope_tail(_mla_q(p, q_norm_w, w_uq, q_head_norm_w), cos, sin)
    k_l, v_l = _mla_kv(p[..., OFF_CKV:OFF_MLA_GATE], kv_norm_w, w_ukv, k_head_norm_w)
    k_l = _rope_tail(k_l, cos, sin)
    attn = _attend_blocked(q, jnp.concatenate([k_c, k_l], axis=1), jnp.concatenate([v_c, v_l], axis=1))
    y = _mixer_concat(p, attn, pool_w, pool_scale, fnet_w) @ w_out
    x = x + gate[:, None, :] * y

    if ctx_out:
        q_c = _mla_q(pc, q_norm_w, w_uq, q_head_norm_w)
        attn_c = _softmax_attend(q_c, k_c, v_c).reshape(ctx.shape[0], ctx.shape[1], MLA_WIDTH)
        yc = _mixer_concat(pc, attn_c, pool_w, pool_scale, fnet_w) @ w_out
        ctx = ctx + gate_c * yc
    return x, ctx


def setup_inputs(seed: int = 0) -> dict:
    key = jax.random.key(seed)
    ks = jax.random.split(key, 20)
    f32 = jnp.float32
    nrm = lambda k, shape, s: jax.random.normal(k, shape, f32) * s
    gpc = POOL_WIDTH // POOL_GROUPS
    gfc = FNET_WIDTH // FNET_HEADS
    return {
        'x': nrm(ks[0], (BATCH, SEQ, D_MODEL), 1.0),
        'c': nrm(ks[1], (BATCH, D_MODEL), 1.0),
        'ctx': nrm(ks[2], (BATCH, CTX_LEN, D_MODEL), 1.0),
        'c_ctx': nrm(ks[3], (D_MODEL,), 1.0),
        'norm_w': 1.0 + nrm(ks[4], (DEPTH, D_MODEL), 0.02),
        'w_ada': nrm(ks[5], (DEPTH, D_MODEL, 3 * D_MODEL), 0.5 * D_MODEL ** -0.5),
        'b_ada': nrm(ks[6], (DEPTH, 3 * D_MODEL), 0.02),
        'w_in': nrm(ks[7], (DEPTH, D_MODEL, D_IN), D_MODEL ** -0.5),
        'pool_w': nrm(ks[8], (DEPTH, POOL_GROUPS, gpc, gpc), gpc ** -0.5),
        'pool_scale': 1.0 + nrm(ks[9], (DEPTH, POOL_WIDTH), 0.1),
        'fnet_w': nrm(ks[10], (DEPTH, FNET_HEADS, gfc, gfc), gfc ** -0.5),
        'q_norm_w': 1.0 + nrm(ks[11], (DEPTH, Q_LORA_RANK), 0.02),
        'w_uq': nrm(ks[12], (DEPTH, Q_LORA_RANK, MLA_HEADS * QK_HEAD_DIM), Q_LORA_RANK ** -0.5),
        'kv_norm_w': 1.0 + nrm(ks[13], (DEPTH, KV_LORA_RANK), 0.02),
        'w_ukv': nrm(ks[14], (DEPTH, KV_LORA_RANK, MLA_HEADS * (QK_NOPE_DIM + V_HEAD_DIM)), KV_LORA_RANK ** -0.5),
        'q_head_norm_w': 1.0 + nrm(ks[15], (DEPTH, QK_HEAD_DIM), 0.02),
        'k_head_norm_w': 1.0 + nrm(ks[16], (DEPTH, QK_HEAD_DIM), 0.02),
        'w_out': nrm(ks[17], (DEPTH, D_MIX, D_MODEL), D_MIX ** -0.5),
    }


def reference(x, c, ctx, c_ctx, norm_w, w_ada, b_ada, w_in, pool_w, pool_scale, fnet_w,
              q_norm_w, w_uq, kv_norm_w, w_ukv, q_head_norm_w, k_head_norm_w, w_out):
    for i in range(DEPTH):
        x, ctx = _layer(x, ctx, c, c_ctx, norm_w[i], w_ada[i], b_ada[i], w_in[i], pool_w[i], pool_scale[i],
                        fnet_w[i], q_norm_w[i], w_uq[i], kv_norm_w[i], w_ukv[i], q_head_norm_w[i],
                        k_head_norm_w[i], w_out[i], i < DEPTH - 1)
    return x
```

```python
import functools
import math

import numpy as np
import jax
import jax.numpy as jnp
from jax import lax
from jax.experimental import pallas as pl
from jax.experimental.pallas import tpu as pltpu

D_MODEL = 2048
DEPTH = 2
GRID_W = 64
POOL_WIDTH = 512
POOL_WINDOWS = (2, 4, 8, 16)
GROUP_CH = 128
N_GROUPS = 4
MLA_HEADS = 8
QK_NOPE_DIM = 128
QK_ROPE_DIM = 64
QK_HEAD_DIM = QK_NOPE_DIM + QK_ROPE_DIM
V_HEAD_DIM = 128
MLA_WIDTH = MLA_HEADS * V_HEAD_DIM
LORA_RANK = 512
N_FREQ_PER_AXIS = QK_ROPE_DIM // 4
ROPE_THETA = 10000.0
EPS = 1e-6
SM_SCALE = QK_HEAD_DIM ** -0.5
LOG2_E = math.log2(math.e)
MAX_SAFE_SHIFT = 60.0

COL_POOL = 0
COL_POOL_GATE = 512
COL_FNET = 1024
COL_FNET_GATE = 1536
COL_CQ = 2048
COL_CKV = 2560
COL_MLA_GATE = 3072
COL_KROPE = 4096
D_IN_PACKED = 4224
HEAD_PAD = 256

LANES = 128
SUBLANES = 8
VMEM_LIMIT = 56 * 1024 * 1024

F32 = jnp.float32
BF16 = jnp.bfloat16

_ROT_SRC = np.array([i + 16 if (i % 32) < 16 else i - 16 for i in range(QK_ROPE_DIM)])
_ROT_SIGN = np.array([-1.0 if (i % 32) < 16 else 1.0 for i in range(QK_ROPE_DIM)], np.float32)


def _silu(x):
    return x * jax.nn.sigmoid(x)


def _dot(a, b):
    return jnp.dot(a, b, preferred_element_type=F32)


def _params(*sem):
    return pltpu.CompilerParams(dimension_semantics=sem, vmem_limit_bytes=VMEM_LIMIT)


def _resident(shape):
    nd = len(shape)
    return pl.BlockSpec(shape, lambda *_: (0,) * nd, pipeline_mode=pl.Buffered(1))


def _ada_kernel(c_ref, w_ref, b_ref, o_ref):
    s = _silu(c_ref[...]).astype(BF16)
    o_ref[0] = _dot(s, w_ref[0].astype(BF16)) + b_ref[0]


def _ada_modulation(cc, w_ada, b_ada):
    tn = 512
    n3 = 3 * D_MODEL
    return pl.pallas_call(
        _ada_kernel,
        out_shape=jax.ShapeDtypeStruct((DEPTH, SUBLANES, n3), F32),
        grid=(DEPTH, n3 // tn),
        in_specs=[
            pl.BlockSpec((SUBLANES, D_MODEL), lambda l, j: (0, 0)),
            pl.BlockSpec((1, D_MODEL, tn), lambda l, j: (l, 0, j)),
            pl.BlockSpec((1, 1, tn), lambda l, j: (l, 0, j)),
        ],
        out_specs=pl.BlockSpec((1, SUBLANES, tn), lambda l, j: (l, 0, j)),
        compiler_params=_params("arbitrary", "arbitrary"),
        name="ada_modulation",
    )(cc, w_ada, b_ada.reshape(DEPTH, 1, n3))


def _in_proj_kernel(x_ref, mod_ref, nw_ref, w_ref, cs_ref, p_ref, uv_ref):
    x = x_ref[0]
    r = lax.rsqrt(jnp.mean(x * x, axis=-1, keepdims=True) + EPS)
    mod = mod_ref[0]
    shift = mod[:, :D_MODEL]
    scale = mod[:, D_MODEL:2 * D_MODEL]
    h = ((x * r * nw_ref[...]) * (1.0 + scale) + shift).astype(BF16)
    chunk = 512
    for c0 in range(0, D_IN_PACKED, chunk):
        c1 = min(c0 + chunk, D_IN_PACKED)
        pc = _dot(h, w_ref[:, c0:c1])
        p_ref[0, :, c0:c1] = pc
        if c0 == COL_FNET:
            g = pc.astype(BF16)
            for hd in range(N_GROUPS):
                uv = _dot(g[:, hd * GROUP_CH:(hd + 1) * GROUP_CH], cs_ref[...])
                uv_ref[0, 0, :, hd * GROUP_CH:(hd + 1) * GROUP_CH] = uv[:, :GROUP_CH].astype(BF16)
                uv_ref[0, 1, :, hd * GROUP_CH:(hd + 1) * GROUP_CH] = uv[:, GROUP_CH:].astype(BF16)


def _in_proj(xin, mod, norm_w, w_packed, cs_tab, *, per_batch_mod, tm):
    b, n, _ = xin.shape
    mod_idx = (lambda bi, i: (bi, 0, 0)) if per_batch_mod else (lambda bi, i: (0, 0, 0))
    return pl.pallas_call(
        _in_proj_kernel,
        out_shape=(jax.ShapeDtypeStruct((b, n, D_IN_PACKED), F32),
                   jax.ShapeDtypeStruct((b, 2, n, POOL_WIDTH), BF16)),
        grid=(b, n // tm),
        in_specs=[
            pl.BlockSpec((1, tm, D_MODEL), lambda bi, i: (bi, i, 0)),
            pl.BlockSpec((1, 1, 3 * D_MODEL), mod_idx),
            _resident((1, D_MODEL)),
            _resident((D_MODEL, D_IN_PACKED)),
            _resident((GROUP_CH, 2 * GROUP_CH)),
        ],
        out_specs=(pl.BlockSpec((1, tm, D_IN_PACKED), lambda bi, i: (bi, i, 0)),
                   pl.BlockSpec((1, 2, tm, POOL_WIDTH), lambda bi, i: (bi, 0, i, 0))),
        compiler_params=_params("arbitrary", "arbitrary"),
        name="in_proj",
    )(xin, mod, norm_w, w_packed, cs_tab)


def _rms(x, w):
    return x * lax.rsqrt(jnp.mean(x * x, axis=-1, keepdims=True) + EPS) * w


def _qkv_kernel(cq_ref, ckv_ref, kr_ref, qnw_ref, kvnw_ref, wuq_ref, wukv_ref, hw_ref,
                cos_ref, sin_ref, q_ref, k_ref, v_ref):
    cos = cos_ref[...]
    sin = sin_ref[...]
    hw = hw_ref[...]
    q_nope_w, q_rope_w, q_rot_w = hw[0:1], hw[1:2], hw[2:3]
    k_nope_w, k_rope_w, k_rot_w = hw[3:4], hw[4:5], hw[5:6]
    q_pad, k_pad = hw[6:7], hw[7:8]
    one_col = (lax.broadcasted_iota(jnp.int32, (cq_ref.shape[1], LANES), 1) == 0).astype(BF16)

    def rope(slab, rope_w, rot_w):
        return slab * (cos * rope_w) + pltpu.roll(slab, LANES // 2, 1) * (sin * rot_w)

    cqn = _rms(cq_ref[0], qnw_ref[...]).astype(BF16)
    for hd in range(MLA_HEADS):
        qh = _dot(cqn, wuq_ref[:, hd * HEAD_PAD:(hd + 1) * HEAD_PAD])
        qn, qs = qh[:, :LANES], qh[:, LANES:]
        ss = jnp.sum(qn * qn, axis=-1, keepdims=True) + 0.5 * jnp.sum(qs * qs, axis=-1, keepdims=True)
        r = lax.rsqrt(ss * (1.0 / QK_HEAD_DIM) + EPS) * (SM_SCALE * LOG2_E)
        q_ref[0, :, hd * HEAD_PAD:hd * HEAD_PAD + LANES] = (qn * r * q_nope_w).astype(BF16)
        q_ref[0, :, hd * HEAD_PAD + LANES:(hd + 1) * HEAD_PAD] = (
            rope(qs, q_rope_w, q_rot_w) * r + q_pad).astype(BF16)

    ckvn = _rms(ckv_ref[0], kvnw_ref[...]).astype(BF16)
    ks = kr_ref[0]
    ss_rope = 0.5 * jnp.sum(ks * ks, axis=-1, keepdims=True)
    k_rope = rope(ks, k_rope_w, k_rot_w)
    for hd in range(MLA_HEADS):
        kn = _dot(ckvn, wukv_ref[:, hd * LANES:(hd + 1) * LANES])
        ss = jnp.sum(kn * kn, axis=-1, keepdims=True) + ss_rope
        r = lax.rsqrt(ss * (1.0 / QK_HEAD_DIM) + EPS)
        k_ref[0, :, hd * HEAD_PAD:hd * HEAD_PAD + LANES] = (kn * r * k_nope_w).astype(BF16)
        k_ref[0, :, hd * HEAD_PAD + LANES:(hd + 1) * HEAD_PAD] = (k_rope * r + k_pad).astype(BF16)
    v = _dot(ckvn, wukv_ref[:, MLA_HEADS * LANES:]).astype(BF16)
    for hd in range(MLA_HEADS):
        v_ref[0, :, hd * HEAD_PAD:hd * HEAD_PAD + LANES] = v[:, hd * LANES:(hd + 1) * LANES]
        v_ref[0, :, hd * HEAD_PAD + LANES:(hd + 1) * HEAD_PAD] = one_col


def _qkv(p, q_norm_w, kv_norm_w, wuq, wukv, head_w, cos_tab, sin_tab, *, tm):
    b, n, _ = p.shape
    row = lambda bi, i: (i, 0)
    return pl.pallas_call(
        _qkv_kernel,
        out_shape=(jax.ShapeDtypeStruct((b, n, MLA_HEADS * HEAD_PAD), BF16),
                   jax.ShapeDtypeStruct((b, n, MLA_HEADS * HEAD_PAD), BF16),
                   jax.ShapeDtypeStruct((b, n, MLA_HEADS * HEAD_PAD), BF16)),
        grid=(b, n // tm),
        in_specs=[
            pl.BlockSpec((1, tm, LORA_RANK), lambda bi, i: (bi, i, COL_CQ // LORA_RANK)),
            pl.BlockSpec((1, tm, LORA_RANK), lambda bi, i: (bi, i, COL_CKV // LORA_RANK)),
            pl.BlockSpec((1, tm, LANES), lambda bi, i: (bi, i, COL_KROPE // LANES)),
            _resident((1, LORA_RANK)),
            _resident((1, LORA_RANK)),
            _resident((LORA_RANK, MLA_HEADS * HEAD_PAD)),
            _resident((LORA_RANK, 2 * MLA_WIDTH)),
            _resident((SUBLANES, LANES)),
            pl.BlockSpec((tm, LANES), row),
            pl.BlockSpec((tm, LANES), row),
        ],
        out_specs=(pl.BlockSpec((1, tm, MLA_HEADS * HEAD_PAD), lambda bi, i: (bi, i, 0)),
                   pl.BlockSpec((1, tm, MLA_HEADS * HEAD_PAD), lambda bi, i: (bi, i, 0)),
                   pl.BlockSpec((1, tm, MLA_HEADS * HEAD_PAD), lambda bi, i: (bi, i, 0))),
        compiler_params=_params("arbitrary", "arbitrary"),
        name="qkv",
    )(p, p, p, q_norm_w, kv_norm_w, wuq, wukv, head_w, cos_tab, sin_tab)


def _scores(q, k):
    return lax.dot_general(q, k, (((1,), (1,)), ((), ())), preferred_element_type=F32)


def _attn_finish(acc, o_ref):
    o_ref[0] = (acc[:, :V_HEAD_DIM] / acc[:, V_HEAD_DIM:V_HEAD_DIM + 1]).astype(BF16)


def _attn_shifted_kernel(*refs, tk, n_lat):
    if n_lat:
        q_ref, kc_ref, vc_ref, kl_ref, vl_ref, o_ref = refs
    else:
        q_ref, kc_ref, vc_ref, o_ref = refs
    q = q_ref[0]

    def tile(k, v):
        return _dot(jnp.exp2(_scores(q, k)).astype(BF16), v)

    acc = tile(kc_ref[0], vc_ref[0])
    for j in range(n_lat // tk):
        acc = acc + tile(kl_ref[0, j * tk:(j + 1) * tk, :], vl_ref[0, j * tk:(j + 1) * tk, :])
    _attn_finish(acc, o_ref)


def _attn_online_kernel(*refs, tk, n_lat):
    if n_lat:
        q_ref, kc_ref, vc_ref, kl_ref, vl_ref, o_ref = refs
    else:
        q_ref, kc_ref, vc_ref, o_ref = refs
    q = q_ref[0]
    tq = q.shape[0]

    def step(k, v, m, acc):
        s = _scores(q, k)
        m_new = jnp.maximum(m, jnp.max(s, axis=-1, keepdims=True))
        p = jnp.exp2(s - m_new)
        return m_new, jnp.exp2(m - m_new) * acc + _dot(p.astype(BF16), v)

    m = jnp.full((tq, 1), -jnp.inf, F32)
    acc = jnp.zeros((tq, HEAD_PAD), F32)
    m, acc = step(kc_ref[0], vc_ref[0], m, acc)
    if n_lat:
        def body(j, carry):
            rows = pl.ds(pl.multiple_of(j * tk, tk), tk)
            return step(kl_ref[0, rows, :], vl_ref[0, rows, :], *carry)
        m, acc = lax.fori_loop(0, n_lat // tk, body, (m, acc))
    _attn_finish(acc, o_ref)


def _attention(shift_is_safe, q, k_ctx, v_ctx, k_lat=None, v_lat=None, *, tq, tk=512):
    b, nq, _ = q.shape
    n_ctx = k_ctx.shape[1]
    n_lat = 0 if k_lat is None else k_lat.shape[1]
    head = lambda bi, hd, i: (bi, 0, hd)
    in_specs = [
        pl.BlockSpec((1, tq, HEAD_PAD), lambda bi, hd, i: (bi, i, hd)),
        pl.BlockSpec((1, n_ctx, HEAD_PAD), head),
        pl.BlockSpec((1, n_ctx, HEAD_PAD), head),
    ]
    args = [q, k_ctx, v_ctx]
    if n_lat:
        in_specs += [pl.BlockSpec((1, n_lat, HEAD_PAD), head)] * 2
        args += [k_lat, v_lat]

    def run(body, name):
        return pl.pallas_call(
            functools.partial(body, tk=tk, n_lat=n_lat),
            out_shape=jax.ShapeDtypeStruct((b, nq, MLA_WIDTH), BF16),
            grid=(b, MLA_HEADS, nq // tq),
            in_specs=in_specs,
            out_specs=pl.BlockSpec((1, tq, V_HEAD_DIM), lambda bi, hd, i: (bi, i, hd)),
            compiler_params=_params("arbitrary", "arbitrary", "arbitrary"),
            name=name,
        )

    return lax.cond(shift_is_safe,
                    lambda *a: run(_attn_shifted_kernel, "attention_shifted")(*a),
                    lambda *a: run(_attn_online_kernel, "attention_online")(*a),
                    *args)


def _pos_dft_kernel(dft_ref, uv_ref, r_ref):
    r_ref[0] = _dot(dft_ref[...], uv_ref[0]).astype(BF16)


def _pos_dft(dft, uv, *, tm):
    b, n2, _ = uv.shape
    n = n2 // 2
    return pl.pallas_call(
        _pos_dft_kernel,
        out_shape=jax.ShapeDtypeStruct((b, n, POOL_WIDTH), BF16),
        grid=(b, n // tm),
        in_specs=[pl.BlockSpec((tm, n2), lambda bi, i: (i, 0)),
                  pl.BlockSpec((1, n2, POOL_WIDTH), lambda bi, i: (bi, 0, 0))],
        out_specs=pl.BlockSpec((1, tm, POOL_WIDTH), lambda bi, i: (bi, i, 0)),
        compiler_params=_params("arbitrary", "arbitrary"),
        name="pos_dft",
    )(dft, uv)


def _out_proj_kernel(x_ref, mod_ref, pu_ref, prev_ref, next_ref, pg_ref, fg_ref, mg_ref, r_ref, at_ref,
                     pw_ref, ps_ref, fw_ref, wo_ref, o_ref, ext_ref, mix_ref, *, tm, n):
    i = pl.program_id(1)
    halo = SUBLANES
    u = pu_ref[0]
    ext_ref[0:halo] = jnp.where(i > 0, prev_ref[0], 0.0)
    ext_ref[halo:halo + tm] = u
    ext_ref[halo + tm:] = jnp.where(i < pl.num_programs(1) - 1, next_ref[0], 0.0)
    t = i * tm + lax.broadcasted_iota(jnp.int32, (tm, 1), 0)
    pg = pg_ref[0]
    for g, w in enumerate(POOL_WINDOWS):
        cols = slice(g * GROUP_CH, (g + 1) * GROUP_CH)
        lo = w // 2
        hi = w - lo - 1
        s = ext_ref[halo - lo:halo - lo + tm, cols]
        for d in range(-lo + 1, hi + 1):
            s = s + ext_ref[halo + d:halo + d + tm, cols]
        cnt = (jnp.minimum(t + hi, n - 1) - jnp.maximum(t - lo, 0) + 1).astype(F32)
        pooled = (s / cnt - u[:, cols]).astype(BF16)
        y = _dot(pooled, pw_ref[g]) * ps_ref[:, cols]
        mix_ref[:, cols] = (_silu(pg[:, cols]) * y).astype(BF16)
    r = r_ref[0]
    fg = fg_ref[0]
    for hd in range(N_GROUPS):
        cols = slice(hd * GROUP_CH, (hd + 1) * GROUP_CH)
        f = _dot(r[:, cols], fw_ref[hd])
        mix_ref[:, POOL_WIDTH + hd * GROUP_CH:POOL_WIDTH + (hd + 1) * GROUP_CH] = (_silu(fg[:, cols]) * f).astype(BF16)
    mix_ref[:, 2 * POOL_WIDTH:] = (_silu(mg_ref[0]) * at_ref[0].astype(F32)).astype(BF16)
    mixed = mix_ref[...]
    gate = mod_ref[0][:, 2 * D_MODEL:]
    chunk = 512
    for c0 in range(0, D_MODEL, chunk):
        y = _dot(mixed, wo_ref[:, c0:c0 + chunk])
        o_ref[0, :, c0:c0 + chunk] = x_ref[0, :, c0:c0 + chunk] + gate[:, c0:c0 + chunk] * y


def _out_proj(xin, mod, p, r, attn, pool_w, pool_scale, fnet_w, w_out, *, per_batch_mod, tm):
    b, n, _ = xin.shape
    mod_idx = (lambda bi, i: (bi, 0, 0)) if per_batch_mod else (lambda bi, i: (0, 0, 0))
    hb = tm // SUBLANES
    last = n // SUBLANES - 1
    pcol = lambda width, off: pl.BlockSpec((1, tm, width), lambda bi, i: (bi, i, off // width))
    return pl.pallas_call(
        functools.partial(_out_proj_kernel, tm=tm, n=n),
        out_shape=jax.ShapeDtypeStruct((b, n, D_MODEL), F32),
        grid=(b, n // tm),
        in_specs=[
            pl.BlockSpec((1, tm, D_MODEL), lambda bi, i: (bi, i, 0)),
            pl.BlockSpec((1, 1, 3 * D_MODEL), mod_idx),
            pcol(POOL_WIDTH, COL_POOL),
            pl.BlockSpec((1, SUBLANES, POOL_WIDTH), lambda bi, i: (bi, jnp.maximum(i * hb - 1, 0), 0)),
            pl.BlockSpec((1, SUBLANES, POOL_WIDTH), lambda bi, i: (bi, jnp.minimum((i + 1) * hb, last), 0)),
            pcol(POOL_WIDTH, COL_POOL_GATE),
            pcol(POOL_WIDTH, COL_FNET_GATE),
            pcol(MLA_WIDTH, COL_MLA_GATE),
            pl.BlockSpec((1, tm, POOL_WIDTH), lambda bi, i: (bi, i, 0)),
            pl.BlockSpec((1, tm, MLA_WIDTH), lambda bi, i: (bi, i, 0)),
            _resident((N_GROUPS, GROUP_CH, GROUP_CH)),
            _resident((1, POOL_WIDTH)),
            _resident((N_GROUPS, GROUP_CH, GROUP_CH)),
            _resident((D_MODEL, D_MODEL)),
        ],
        out_specs=pl.BlockSpec((1, tm, D_MODEL), lambda bi, i: (bi, i, 0)),
        scratch_shapes=[pltpu.VMEM((tm + 2 * SUBLANES, POOL_WIDTH), F32),
                        pltpu.VMEM((tm, D_MODEL), BF16)],
        compiler_params=_params("arbitrary", "arbitrary"),
        name="out_proj",
    )(xin, mod, p, p, p, p, p, p, r, attn, pool_w, pool_scale, fnet_w, w_out)


def _rope_tables(n):
    t = jnp.arange(n, dtype=jnp.int32)
    inv_freq = ROPE_THETA ** (-jnp.arange(N_FREQ_PER_AXIS, dtype=F32) / N_FREQ_PER_AXIS)
    ang_r = (t // GRID_W).astype(F32)[:, None] * inv_freq
    ang_c = (t % GRID_W).astype(F32)[:, None] * inv_freq
    ang = jnp.concatenate([ang_r, ang_r, ang_c, ang_c], axis=-1)
    pad = jnp.zeros((n, LANES - QK_ROPE_DIM), F32)
    return (jnp.concatenate([jnp.cos(ang), pad], axis=-1),
            jnp.concatenate([jnp.sin(ang) * _ROT_SIGN, pad], axis=-1))


def _identity_rope_tables(n):
    pad = jnp.zeros((n, LANES - QK_ROPE_DIM), F32)
    return jnp.concatenate([jnp.ones((n, QK_ROPE_DIM), F32), pad], axis=-1), jnp.zeros((n, LANES), F32)


def _channel_dft_table():
    k = np.arange(GROUP_CH)
    ang = 2.0 * np.pi * ((k[:, None] * k[None, :]) % GROUP_CH) / GROUP_CH
    tab = np.concatenate([np.cos(ang), np.sin(ang)], axis=1) / math.sqrt(GROUP_CH)
    return jnp.asarray(tab, F32).astype(BF16)


def _position_dft_table(n):
    f = 1
    while f * f < n:
        f *= 2
    g = n // f
    k = jnp.arange(n, dtype=jnp.int32)
    a = jnp.arange(f, dtype=jnp.int32)
    bb = jnp.arange(g, dtype=jnp.int32)
    ang_a = ((a[:, None] * k[None, :]) % f).astype(F32) * (2.0 * math.pi / f)
    ang_b = ((bb[:, None] * k[None, :]) % n).astype(F32) * (2.0 * math.pi / n)
    ca, sa = jnp.cos(ang_a)[:, None, :], jnp.sin(ang_a)[:, None, :]
    cb, sb = jnp.cos(ang_b)[None, :, :], jnp.sin(ang_b)[None, :, :]
    scale = 1.0 / math.sqrt(n)
    cos = ((ca * cb - sa * sb) * scale).reshape(n, n)
    sin = ((sa * cb + ca * sb) * scale).reshape(n, n)
    return jnp.concatenate([cos, -sin], axis=1).astype(BF16)


def _pack_w_in(w):
    kr = w[:, 3072:3136]
    return jnp.concatenate([w[:, :3072], w[:, 3136:], kr, kr[:, _ROT_SRC]], axis=1).astype(BF16)


def _pack_w_uq(w):
    w = w.reshape(LORA_RANK, MLA_HEADS, QK_HEAD_DIM)
    rope = w[:, :, QK_NOPE_DIM:]
    return jnp.concatenate([w[:, :, :QK_NOPE_DIM], rope, rope[:, :, _ROT_SRC]], axis=-1).reshape(
        LORA_RANK, MLA_HEADS * HEAD_PAD).astype(BF16)


def _pack_w_ukv(w):
    w = w.reshape(LORA_RANK, MLA_HEADS, QK_NOPE_DIM + V_HEAD_DIM)
    return jnp.concatenate([w[:, :, :QK_NOPE_DIM].reshape(LORA_RANK, -1),
                            w[:, :, QK_NOPE_DIM:].reshape(LORA_RANK, -1)], axis=1).astype(BF16)


def _pack_head_w(qw, kw):
    pad = jnp.zeros((LANES - QK_ROPE_DIM,), F32)
    rows = []
    for w in (qw, kw):
        rope = w[QK_NOPE_DIM:]
        rows += [w[:QK_NOPE_DIM], jnp.concatenate([rope, pad]), jnp.concatenate([rope[_ROT_SRC], pad])]
    shift = QK_HEAD_DIM * SM_SCALE * LOG2_E * jnp.max(jnp.abs(qw)) * jnp.max(jnp.abs(kw))
    spare = (np.arange(LANES) == QK_ROPE_DIM).astype(np.float32)
    rows += [-shift * spare, jnp.asarray(spare)]
    return jnp.stack(rows), shift <= MAX_SAFE_SHIFT


def kernel(x, c, ctx, c_ctx, norm_w, w_ada, b_ada, w_in, pool_w, pool_scale, fnet_w, q_norm_w, w_uq,
           kv_norm_w, w_ukv, q_head_norm_w, k_head_norm_w, w_out):
    batch, seq, _ = x.shape
    n_ctx = ctx.shape[1]
    cc = jnp.concatenate([c, c_ctx[None, :], jnp.zeros((SUBLANES - batch - 1, D_MODEL), F32)], axis=0)
    mod_all = _ada_modulation(cc, w_ada, b_ada)

    cs_tab = _channel_dft_table()
    rope_lat = _rope_tables(seq)
    rope_ctx = _identity_rope_tables(n_ctx)
    dft_lat = _position_dft_table(seq)
    dft_ctx = _position_dft_table(n_ctx)
    tm_lat, tm_ctx = 256, n_ctx

    for l in range(DEPTH):
        mod_lat = mod_all[l, :batch][:, None, :]
        mod_ctx = mod_all[l, batch:batch + 1][:, None, :]
        nw = norm_w[l][None, :]
        w_in_p = _pack_w_in(w_in[l])
        wuq = _pack_w_uq(w_uq[l])
        wukv = _pack_w_ukv(w_ukv[l])
        head_w, shift_is_safe = _pack_head_w(q_head_norm_w[l], k_head_norm_w[l])
        qnw, kvnw = q_norm_w[l][None, :], kv_norm_w[l][None, :]
        pw, fw = pool_w[l].astype(BF16), fnet_w[l].astype(BF16)
        ps = pool_scale[l][None, :]
        wo = w_out[l].astype(BF16)
        ctx_out = l < DEPTH - 1

        p_c, uv_c = _in_proj(ctx, mod_ctx, nw, w_in_p, cs_tab, per_batch_mod=False, tm=tm_ctx)
        q_c, k_c, v_c = _qkv(p_c, qnw, kvnw, wuq, wukv, head_w, *rope_ctx, tm=tm_ctx)
        p_l, uv_l = _in_proj(x, mod_lat, nw, w_in_p, cs_tab, per_batch_mod=True, tm=tm_lat)
        q_l, k_l, v_l = _qkv(p_l, qnw, kvnw, wuq, wukv, head_w, *rope_lat, tm=tm_lat)

        attn_l = _attention(shift_is_safe, q_l, k_c, v_c, k_l, v_l, tq=512)
        r_l = _pos_dft(dft_lat, uv_l.reshape(batch, 2 * seq, POOL_WIDTH), tm=512)
        x_new = _out_proj(x, mod_lat, p_l, r_l, attn_l, pw, ps, fw, wo, per_batch_mod=True, tm=tm_lat)
        if ctx_out:
            attn_c = _attention(shift_is_safe, q_c, k_c, v_c, tq=n_ctx)
            r_c = _pos_dft(dft_ctx, uv_c.reshape(batch, 2 * n_ctx, POOL_WIDTH), tm=n_ctx)
            ctx = _out_proj(ctx, mod_ctx, p_c, r_c, attn_c, pw, ps, fw, wo, per_batch_mod=False, tm=tm_ctx)
        x = x_new
    return x
```

```python
import functools
import math

import numpy as np
import jax
import jax.numpy as jnp
from jax import lax
from jax.experimental import pallas as pl
from jax.experimental.pallas import tpu as pltpu

D_MODEL = 2048
DEPTH = 2
GRID_W = 64
POOL_WIDTH = 512
POOL_WINDOWS = (2, 4, 8, 16)
GROUP_CH = 128
N_GROUPS = 4
MLA_HEADS = 8
QK_NOPE_DIM = 128
QK_ROPE_DIM = 64
QK_HEAD_DIM = QK_NOPE_DIM + QK_ROPE_DIM
V_HEAD_DIM = 128
MLA_WIDTH = MLA_HEADS * V_HEAD_DIM
LORA_RANK = 512
N_FREQ_PER_AXIS = QK_ROPE_DIM // 4
ROPE_THETA = 10000.0
EPS = 1e-6
SM_SCALE = QK_HEAD_DIM ** -0.5
LOG2_E = math.log2(math.e)
MAX_SAFE_SHIFT = 60.0

COL_POOL = 0
COL_POOL_GATE = 512
COL_FNET = 1024
COL_FNET_GATE = 1536
COL_CQ = 2048
COL_CKV = 2560
COL_MLA_GATE = 3072
COL_KROPE = 4096
D_IN_PACKED = 4224
HEAD_PAD = 256

LANES = 128
SUBLANES = 8
VMEM_LIMIT = 56 * 1024 * 1024
ROW_SUB = 256
POOL_HALO = 2 * SUBLANES
BAND_K = ROW_SUB + LANES

F32 = jnp.float32
BF16 = jnp.bfloat16

_ROT_SRC = np.array([i + 16 if (i % 32) < 16 else i - 16 for i in range(QK_ROPE_DIM)])
_ROT_SIGN = np.array([-1.0 if (i % 32) < 16 else 1.0 for i in range(QK_ROPE_DIM)], np.float32)


def _silu(x):
    return x * jax.nn.sigmoid(x)


def _dot(a, b):
    return jnp.dot(a, b, preferred_element_type=F32)


def _params(*sem):
    return pltpu.CompilerParams(dimension_semantics=sem, vmem_limit_bytes=VMEM_LIMIT)


def _resident(shape, layer=None):
    nd = len(shape)
    if layer is None:
        return pl.BlockSpec(shape, lambda *_: (0,) * nd, pipeline_mode=pl.Buffered(1))
    return pl.BlockSpec((None,) + tuple(shape), lambda *_: (layer,) + (0,) * nd, pipeline_mode=pl.Buffered(1))


def _ada_kernel(c_ref, w_ref, b_ref, o_ref):
    s = _silu(c_ref[...]).astype(BF16)
    o_ref[0] = _dot(s, w_ref[0].astype(BF16)) + b_ref[0]


def _ada_modulation(cc, w_ada, b_ada):
    tn = 512
    n3 = 3 * D_MODEL
    return pl.pallas_call(
        _ada_kernel,
        out_shape=jax.ShapeDtypeStruct((DEPTH, SUBLANES, n3), F32),
        grid=(DEPTH, n3 // tn),
        in_specs=[
            pl.BlockSpec((SUBLANES, D_MODEL), lambda l, j: (0, 0)),
            pl.BlockSpec((1, D_MODEL, tn), lambda l, j: (l, 0, j)),
            pl.BlockSpec((1, 1, tn), lambda l, j: (l, 0, j)),
        ],
        out_specs=pl.BlockSpec((1, SUBLANES, tn), lambda l, j: (l, 0, j)),
        compiler_params=_params("arbitrary", "arbitrary"),
        name="ada_modulation",
    )(cc, w_ada, b_ada.reshape(DEPTH, 1, n3))


def _in_proj_kernel(x_ref, mod_ref, nw_ref, w_ref, cs_ref, p_ref, uv_ref):
    mod = mod_ref[0]
    shift = mod[:, :D_MODEL]
    scale1 = (1.0 + mod[:, D_MODEL:2 * D_MODEL]) * nw_ref[...]
    tm = x_ref.shape[1]
    for r0 in range(0, tm, ROW_SUB):
        rows = slice(r0, min(r0 + ROW_SUB, tm))
        x = x_ref[0, rows, :]
        r = lax.rsqrt(jnp.mean(x * x, axis=-1, keepdims=True) + EPS)
        h = (x * r * scale1 + shift).astype(BF16)
        chunk = 512
        for c0 in range(0, D_IN_PACKED, chunk):
            c1 = min(c0 + chunk, D_IN_PACKED)
            pc = _dot(h, w_ref[:, c0:c1])
            p_ref[0, rows, c0:c1] = pc
            if c0 == COL_FNET:
                g = pc.astype(BF16)
                for hd in range(N_GROUPS):
                    cols = slice(hd * GROUP_CH, (hd + 1) * GROUP_CH)
                    uv = _dot(g[:, cols], cs_ref[...])
                    uv_ref[0, 0, rows, cols] = uv[:, :GROUP_CH].astype(BF16)
                    uv_ref[0, 1, rows, cols] = uv[:, GROUP_CH:].astype(BF16)


def _in_proj(xin, mod, norm_w, w_packed, cs_tab, *, layer, per_batch_mod, tm):
    b, n, _ = xin.shape
    mod_idx = (lambda bi, i: (bi, 0, 0)) if per_batch_mod else (lambda bi, i: (0, 0, 0))
    return pl.pallas_call(
        _in_proj_kernel,
        out_shape=(jax.ShapeDtypeStruct((b, n, D_IN_PACKED), F32),
                   jax.ShapeDtypeStruct((b, 2, n, POOL_WIDTH), BF16)),
        grid=(b, n // tm),
        in_specs=[
            pl.BlockSpec((1, tm, D_MODEL), lambda bi, i: (bi, i, 0)),
            pl.BlockSpec((1, 1, 3 * D_MODEL), mod_idx),
            _resident((1, D_MODEL)),
            _resident((D_MODEL, D_IN_PACKED), layer),
            _resident((GROUP_CH, 2 * GROUP_CH)),
        ],
        out_specs=(pl.BlockSpec((1, tm, D_IN_PACKED), lambda bi, i: (bi, i, 0)),
                   pl.BlockSpec((1, 2, tm, POOL_WIDTH), lambda bi, i: (bi, 0, i, 0))),
        compiler_params=_params("arbitrary", "arbitrary"),
        name="in_proj",
    )(xin, mod, norm_w, w_packed, cs_tab)


def _rms(x, w):
    return x * lax.rsqrt(jnp.mean(x * x, axis=-1, keepdims=True) + EPS) * w


def _qkv_kernel(cq_ref, ckv_ref, kr_ref, qnw_ref, kvnw_ref, wuq_ref, wukv_ref, hw_ref,
                cos_ref, sin_ref, q_ref, k_ref, v_ref):
    cos = cos_ref[...]
    sin = sin_ref[...]
    hw = hw_ref[...]
    q_nope_w, q_rope_w, q_rot_w = hw[0:1], hw[1:2], hw[2:3]
    k_nope_w, k_rope_w, k_rot_w = hw[3:4], hw[4:5], hw[5:6]
    q_pad, k_pad = hw[6:7], hw[7:8]
    one_col = (lax.broadcasted_iota(jnp.int32, (cq_ref.shape[1], LANES), 1) == 0).astype(BF16)

    def rope(slab, rope_w, rot_w):
        return slab * (cos * rope_w) + pltpu.roll(slab, LANES // 2, 1) * (sin * rot_w)

    cqn = _rms(cq_ref[0], qnw_ref[...]).astype(BF16)
    for hd in range(MLA_HEADS):
        qh = _dot(cqn, wuq_ref[:, hd * HEAD_PAD:(hd + 1) * HEAD_PAD])
        qn, qs = qh[:, :LANES], qh[:, LANES:]
        ss = jnp.sum(qn * qn, axis=-1, keepdims=True) + 0.5 * jnp.sum(qs * qs, axis=-1, keepdims=True)
        r = lax.rsqrt(ss * (1.0 / QK_HEAD_DIM) + EPS) * (SM_SCALE * LOG2_E)
        q_ref[0, :, hd * HEAD_PAD:hd * HEAD_PAD + LANES] = (qn * r * q_nope_w).astype(BF16)
        q_ref[0, :, hd * HEAD_PAD + LANES:(hd + 1) * HEAD_PAD] = (
            rope(qs, q_rope_w, q_rot_w) * r + q_pad).astype(BF16)

    ckvn = _rms(ckv_ref[0], kvnw_ref[...]).astype(BF16)
    ks = kr_ref[0]
    ss_rope = 0.5 * jnp.sum(ks * ks, axis=-1, keepdims=True)
    k_rope = rope(ks, k_rope_w, k_rot_w)
    for hd in range(MLA_HEADS):
        kn = _dot(ckvn, wukv_ref[:, hd * LANES:(hd + 1) * LANES])
        ss = jnp.sum(kn * kn, axis=-1, keepdims=True) + ss_rope
        r = lax.rsqrt(ss * (1.0 / QK_HEAD_DIM) + EPS)
        k_ref[0, :, hd * HEAD_PAD:hd * HEAD_PAD + LANES] = (kn * r * k_nope_w).astype(BF16)
        k_ref[0, :, hd * HEAD_PAD + LANES:(hd + 1) * HEAD_PAD] = (k_rope * r + k_pad).astype(BF16)
    v = _dot(ckvn, wukv_ref[:, MLA_HEADS * LANES:]).astype(BF16)
    for hd in range(MLA_HEADS):
        v_ref[0, :, hd * HEAD_PAD:hd * HEAD_PAD + LANES] = v[:, hd * LANES:(hd + 1) * LANES]
        v_ref[0, :, hd * HEAD_PAD + LANES:(hd + 1) * HEAD_PAD] = one_col


def _qkv(p, q_norm_w, kv_norm_w, wuq, wukv, head_w, cos_tab, sin_tab, *, layer, tm):
    b, n, _ = p.shape
    row = lambda bi, i: (i, 0)
    return pl.pallas_call(
        _qkv_kernel,
        out_shape=(jax.ShapeDtypeStruct((b, n, MLA_HEADS * HEAD_PAD), BF16),
                   jax.ShapeDtypeStruct((b, n, MLA_HEADS * HEAD_PAD), BF16),
                   jax.ShapeDtypeStruct((b, n, MLA_HEADS * HEAD_PAD), BF16)),
        grid=(b, n // tm),
        in_specs=[
            pl.BlockSpec((1, tm, LORA_RANK), lambda bi, i: (bi, i, COL_CQ // LORA_RANK)),
            pl.BlockSpec((1, tm, LORA_RANK), lambda bi, i: (bi, i, COL_CKV // LORA_RANK)),
            pl.BlockSpec((1, tm, LANES), lambda bi, i: (bi, i, COL_KROPE // LANES)),
            _resident((1, LORA_RANK)),
            _resident((1, LORA_RANK)),
            _resident((LORA_RANK, MLA_HEADS * HEAD_PAD), layer),
            _resident((LORA_RANK, 2 * MLA_WIDTH), layer),
            _resident((SUBLANES, LANES)),
            pl.BlockSpec((tm, LANES), row),
            pl.BlockSpec((tm, LANES), row),
        ],
        out_specs=(pl.BlockSpec((1, tm, MLA_HEADS * HEAD_PAD), lambda bi, i: (bi, i, 0)),
                   pl.BlockSpec((1, tm, MLA_HEADS * HEAD_PAD), lambda bi, i: (bi, i, 0)),
                   pl.BlockSpec((1, tm, MLA_HEADS * HEAD_PAD), lambda bi, i: (bi, i, 0))),
        compiler_params=_params("arbitrary", "arbitrary"),
        name="qkv",
    )(p, p, p, q_norm_w, kv_norm_w, wuq, wukv, head_w, cos_tab, sin_tab)


def _scores(q, k):
    return lax.dot_general(q, k, (((1,), (1,)), ((), ())), preferred_element_type=F32)


def _attn_finish(acc, o_ref):
    o_ref[0] = (acc[:, :V_HEAD_DIM] / acc[:, V_HEAD_DIM:V_HEAD_DIM + 1]).astype(BF16)


def _attn_shifted_kernel(*refs, tk, n_lat):
    if n_lat:
        q_ref, kc_ref, vc_ref, kl_ref, vl_ref, o_ref = refs
    else:
        q_ref, kc_ref, vc_ref, o_ref = refs
    q = q_ref[0]

    def tile(k, v):
        return _dot(jnp.exp2(_scores(q, k)).astype(BF16), v)

    acc = tile(kc_ref[0], vc_ref[0])
    for j in range(n_lat // tk):
        acc = acc + tile(kl_ref[0, j * tk:(j + 1) * tk, :], vl_ref[0, j * tk:(j + 1) * tk, :])
    _attn_finish(acc, o_ref)


def _attn_online_kernel(*refs, tk, n_lat):
    if n_lat:
        q_ref, kc_ref, vc_ref, kl_ref, vl_ref, o_ref = refs
    else:
        q_ref, kc_ref, vc_ref, o_ref = refs
    q = q_ref[0]
    tq = q.shape[0]

    def step(k, v, m, acc):
        s = _scores(q, k)
        m_new = jnp.maximum(m, jnp.max(s, axis=-1, keepdims=True))
        p = jnp.exp2(s - m_new)
        return m_new, jnp.exp2(m - m_new) * acc + _dot(p.astype(BF16), v)

    m = jnp.full((tq, 1), -jnp.inf, F32)
    acc = jnp.zeros((tq, HEAD_PAD), F32)
    m, acc = step(kc_ref[0], vc_ref[0], m, acc)
    if n_lat:
        def body(j, carry):
            rows = pl.ds(pl.multiple_of(j * tk, tk), tk)
            return step(kl_ref[0, rows, :], vl_ref[0, rows, :], *carry)
        m, acc = lax.fori_loop(0, n_lat // tk, body, (m, acc))
    _attn_finish(acc, o_ref)


def _attention(shift_is_safe, q, k_ctx, v_ctx, k_lat=None, v_lat=None, *, tq, tk=512):
    b, nq, _ = q.shape
    n_ctx = k_ctx.shape[1]
    n_lat = 0 if k_lat is None else k_lat.shape[1]
    head = lambda bi, hd, i: (bi, 0, hd)
    in_specs = [
        pl.BlockSpec((1, tq, HEAD_PAD), lambda bi, hd, i: (bi, i, hd)),
        pl.BlockSpec((1, n_ctx, HEAD_PAD), head),
        pl.BlockSpec((1, n_ctx, HEAD_PAD), head),
    ]
    args = [q, k_ctx, v_ctx]
    if n_lat:
        in_specs += [pl.BlockSpec((1, n_lat, HEAD_PAD), head)] * 2
        args += [k_lat, v_lat]

    def run(body, name):
        return pl.pallas_call(
            functools.partial(body, tk=tk, n_lat=n_lat),
            out_shape=jax.ShapeDtypeStruct((b, nq, MLA_WIDTH), BF16),
            grid=(b, MLA_HEADS, nq // tq),
            in_specs=in_specs,
            out_specs=pl.BlockSpec((1, tq, V_HEAD_DIM), lambda bi, hd, i: (bi, i, hd)),
            compiler_params=_params("arbitrary", "arbitrary", "arbitrary"),
            name=name,
        )

    return lax.cond(shift_is_safe,
                    lambda *a: run(_attn_shifted_kernel, "attention_shifted")(*a),
                    lambda *a: run(_attn_online_kernel, "attention_online")(*a),
                    *args)


def _pos_dft_kernel(dft_ref, uv_ref, r_ref):
    r_ref[0] = _dot(dft_ref[...], uv_ref[0]).astype(BF16)


def _pos_dft(dft, uv, *, tm):
    b, n2, _ = uv.shape
    n = n2 // 2
    return pl.pallas_call(
        _pos_dft_kernel,
        out_shape=jax.ShapeDtypeStruct((b, n, POOL_WIDTH), BF16),
        grid=(b, n // tm),
        in_specs=[pl.BlockSpec((tm, n2), lambda bi, i: (i, 0)),
                  pl.BlockSpec((1, n2, POOL_WIDTH), lambda bi, i: (bi, 0, 0))],
        out_specs=pl.BlockSpec((1, tm, POOL_WIDTH), lambda bi, i: (bi, i, 0)),
        compiler_params=_params("arbitrary", "arbitrary"),
        name="pos_dft",
    )(dft, uv)


def _out_proj_kernel(x_ref, mod_ref, pu_ref, prev_ref, next_ref, pg_ref, fg_ref, mg_ref, r_ref, at_ref,
                     band_ref, pw_ref, ps_ref, fw_ref, wo_ref, o_ref, ext_ref, *, tm, n):
    i = pl.program_id(1)
    ext_ref[0:POOL_HALO] = jnp.where(i > 0, prev_ref[0], 0.0).astype(BF16)
    ext_ref[POOL_HALO:POOL_HALO + tm] = pu_ref[0].astype(BF16)
    ext_ref[POOL_HALO + tm:2 * POOL_HALO + tm] = jnp.where(i < pl.num_programs(1) - 1, next_ref[0], 0.0).astype(BF16)
    ext_ref[2 * POOL_HALO + tm:] = jnp.zeros((BAND_K - ROW_SUB - 2 * POOL_HALO, POOL_WIDTH), BF16)
    gate = mod_ref[0][:, 2 * D_MODEL:]
    pair = 2 * GROUP_CH
    for r0 in range(0, tm, ROW_SUB):
        rows = slice(r0, r0 + ROW_SUB)
        t = i * tm + r0 + lax.broadcasted_iota(jnp.int32, (ROW_SUB, 1), 0)
        pooled = []
        for g, w in enumerate(POOL_WINDOWS):
            cols = slice(g * GROUP_CH, (g + 1) * GROUP_CH)
            lo = w // 2
            hi = w - lo - 1
            s = _dot(band_ref[g], ext_ref[r0:r0 + BAND_K, cols])
            cnt = (jnp.minimum(t + hi, n - 1) - jnp.maximum(t - lo, 0) + 1).astype(F32)
            pooled.append((s / cnt - pu_ref[0, rows, cols]).astype(BF16))
        parts = []
        for j in range(N_GROUPS // 2):
            cols = slice(j * pair, (j + 1) * pair)
            y = _dot(jnp.concatenate(pooled[2 * j:2 * j + 2], axis=-1), pw_ref[j]) * ps_ref[:, cols]
            parts.append((_silu(pg_ref[0, rows, cols]) * y).astype(BF16))
        for j in range(N_GROUPS // 2):
            cols = slice(j * pair, (j + 1) * pair)
            f = _dot(r_ref[0, rows, cols], fw_ref[j])
            parts.append((_silu(fg_ref[0, rows, cols]) * f).astype(BF16))
        parts.append((_silu(mg_ref[0, rows, :]) * at_ref[0, rows, :].astype(F32)).astype(BF16))
        mixed = jnp.concatenate(parts, axis=-1)
        chunk = 512
        for c0 in range(0, D_MODEL, chunk):
            y = _dot(mixed, wo_ref[:, c0:c0 + chunk])
            o_ref[0, rows, c0:c0 + chunk] = x_ref[0, rows, c0:c0 + chunk] + gate[:, c0:c0 + chunk] * y


def _out_proj(xin, mod, p, r, attn, pool_w, pool_scale, fnet_w, w_out, *, layer, per_batch_mod, tm):
    b, n, _ = xin.shape
    mod_idx = (lambda bi, i: (bi, 0, 0)) if per_batch_mod else (lambda bi, i: (0, 0, 0))
    hb = tm // POOL_HALO
    last = n // POOL_HALO - 1
    pcol = lambda width, off: pl.BlockSpec((1, tm, width), lambda bi, i: (bi, i, off // width))
    pair = 2 * GROUP_CH
    return pl.pallas_call(
        functools.partial(_out_proj_kernel, tm=tm, n=n),
        out_shape=jax.ShapeDtypeStruct((b, n, D_MODEL), F32),
        grid=(b, n // tm),
        in_specs=[
            pl.BlockSpec((1, tm, D_MODEL), lambda bi, i: (bi, i, 0)),
            pl.BlockSpec((1, 1, 3 * D_MODEL), mod_idx),
            pcol(POOL_WIDTH, COL_POOL),
            pl.BlockSpec((1, POOL_HALO, POOL_WIDTH), lambda bi, i: (bi, jnp.maximum(i * hb - 1, 0), 0)),
            pl.BlockSpec((1, POOL_HALO, POOL_WIDTH), lambda bi, i: (bi, jnp.minimum((i + 1) * hb, last), 0)),
            pcol(POOL_WIDTH, COL_POOL_GATE),
            pcol(POOL_WIDTH, COL_FNET_GATE),
            pcol(MLA_WIDTH, COL_MLA_GATE),
            pl.BlockSpec((1, tm, POOL_WIDTH), lambda bi, i: (bi, i, 0)),
            pl.BlockSpec((1, tm, MLA_WIDTH), lambda bi, i: (bi, i, 0)),
            _resident((N_GROUPS, ROW_SUB, BAND_K)),
            _resident((N_GROUPS // 2, pair, pair), layer),
            _resident((1, POOL_WIDTH)),
            _resident((N_GROUPS // 2, pair, pair), layer),
            _resident((D_MODEL, D_MODEL), layer),
        ],
        out_specs=pl.BlockSpec((1, tm, D_MODEL), lambda bi, i: (bi, i, 0)),
        scratch_shapes=[pltpu.VMEM((tm + BAND_K - ROW_SUB, POOL_WIDTH), BF16)],
        compiler_params=_params("arbitrary", "arbitrary"),
        name="out_proj",
    )(xin, mod, p, p, p, p, p, p, r, attn, _pool_band_table(), pool_w, pool_scale, fnet_w, w_out)


def _rope_tables(n):
    t = jnp.arange(n, dtype=jnp.int32)
    inv_freq = ROPE_THETA ** (-jnp.arange(N_FREQ_PER_AXIS, dtype=F32) / N_FREQ_PER_AXIS)
    ang_r = (t // GRID_W).astype(F32)[:, None] * inv_freq
    ang_c = (t % GRID_W).astype(F32)[:, None] * inv_freq
    ang = jnp.concatenate([ang_r, ang_r, ang_c, ang_c], axis=-1)
    pad = jnp.zeros((n, LANES - QK_ROPE_DIM), F32)
    return (jnp.concatenate([jnp.cos(ang), pad], axis=-1),
            jnp.concatenate([jnp.sin(ang) * _ROT_SIGN, pad], axis=-1))


def _identity_rope_tables(n):
    pad = jnp.zeros((n, LANES - QK_ROPE_DIM), F32)
    return jnp.concatenate([jnp.ones((n, QK_ROPE_DIM), F32), pad], axis=-1), jnp.zeros((n, LANES), F32)


def _channel_dft_table():
    k = np.arange(GROUP_CH)
    ang = 2.0 * np.pi * ((k[:, None] * k[None, :]) % GROUP_CH) / GROUP_CH
    tab = np.concatenate([np.cos(ang), np.sin(ang)], axis=1) / math.sqrt(GROUP_CH)
    return jnp.asarray(tab, F32).astype(BF16)


def _position_dft_table(n):
    f = 1
    while f * f < n:
        f *= 2
    g = n // f
    k = jnp.arange(n, dtype=jnp.int32)
    a = jnp.arange(f, dtype=jnp.int32)
    bb = jnp.arange(g, dtype=jnp.int32)
    ang_a = ((a[:, None] * k[None, :]) % f).astype(F32) * (2.0 * math.pi / f)
    ang_b = ((bb[:, None] * k[None, :]) % n).astype(F32) * (2.0 * math.pi / n)
    ca, sa = jnp.cos(ang_a)[:, None, :], jnp.sin(ang_a)[:, None, :]
    cb, sb = jnp.cos(ang_b)[None, :, :], jnp.sin(ang_b)[None, :, :]
    scale = 1.0 / math.sqrt(n)
    cos = ((ca * cb - sa * sb) * scale).reshape(n, n)
    sin = ((sa * cb + ca * sb) * scale).reshape(n, n)
    return jnp.concatenate([cos, -sin], axis=1).astype(BF16)


def _pool_band_table():
    t = np.arange(ROW_SUB)[:, None]
    d = np.arange(BAND_K)[None, :] - POOL_HALO - t
    bands = [((d >= -(w // 2)) & (d <= w - w // 2 - 1)).astype(np.float32) for w in POOL_WINDOWS]
    return jnp.asarray(np.stack(bands), BF16)


def _pair_block_diag(w):
    even, odd = w[:, 0::2], w[:, 1::2]
    zero = jnp.zeros_like(even)
    return jnp.concatenate([jnp.concatenate([even, zero], axis=-1),
                            jnp.concatenate([zero, odd], axis=-1)], axis=-2).astype(BF16)


def _pack_w_in(w):
    kr = w[..., 3072:3136]
    return jnp.concatenate([w[..., :3072], w[..., 3136:], kr, kr[..., _ROT_SRC]], axis=-1).astype(BF16)


def _pack_w_uq(w):
    w = w.reshape(DEPTH, LORA_RANK, MLA_HEADS, QK_HEAD_DIM)
    rope = w[..., QK_NOPE_DIM:]
    return jnp.concatenate([w[..., :QK_NOPE_DIM], rope, rope[..., _ROT_SRC]], axis=-1).reshape(
        DEPTH, LORA_RANK, MLA_HEADS * HEAD_PAD).astype(BF16)


def _pack_w_ukv(w):
    w = w.reshape(DEPTH, LORA_RANK, MLA_HEADS, QK_NOPE_DIM + V_HEAD_DIM)
    return jnp.concatenate([w[..., :QK_NOPE_DIM].reshape(DEPTH, LORA_RANK, -1),
                            w[..., QK_NOPE_DIM:].reshape(DEPTH, LORA_RANK, -1)], axis=-1).astype(BF16)


def _pack_head_w(qw, kw):
    pad = jnp.zeros((LANES - QK_ROPE_DIM,), F32)
    rows = []
    for w in (qw, kw):
        rope = w[QK_NOPE_DIM:]
        rows += [w[:QK_NOPE_DIM], jnp.concatenate([rope, pad]), jnp.concatenate([rope[_ROT_SRC], pad])]
    shift = QK_HEAD_DIM * SM_SCALE * LOG2_E * jnp.max(jnp.abs(qw)) * jnp.max(jnp.abs(kw))
    spare = (np.arange(LANES) == QK_ROPE_DIM).astype(np.float32)
    rows += [-shift * spare, jnp.asarray(spare)]
    return jnp.stack(rows), shift <= MAX_SAFE_SHIFT


def kernel(x, c, ctx, c_ctx, norm_w, w_ada, b_ada, w_in, pool_w, pool_scale, fnet_w, q_norm_w, w_uq,
           kv_norm_w, w_ukv, q_head_norm_w, k_head_norm_w, w_out):
    batch, seq, _ = x.shape
    n_ctx = ctx.shape[1]
    cc = jnp.concatenate([c, c_ctx[None, :], jnp.zeros((SUBLANES - batch - 1, D_MODEL), F32)], axis=0)
    mod_all = _ada_modulation(cc, w_ada, b_ada)

    cs_tab = _channel_dft_table()
    rope_lat = _rope_tables(seq)
    rope_ctx = _identity_rope_tables(n_ctx)
    dft_lat = _position_dft_table(seq)
    dft_ctx = _position_dft_table(n_ctx)
    tm_lat, tm_qkv, tm_ctx = 512, 256, n_ctx

    w_in_p, wuq, wukv = _pack_w_in(w_in), _pack_w_uq(w_uq), _pack_w_ukv(w_ukv)
    pw, fw, wo = _pair_block_diag(pool_w), _pair_block_diag(fnet_w), w_out.astype(BF16)

    for l in range(DEPTH):
        mod_lat = mod_all[l, :batch][:, None, :]
        mod_ctx = mod_all[l, batch:batch + 1][:, None, :]
        nw = norm_w[l][None, :]
        head_w, shift_is_safe = _pack_head_w(q_head_norm_w[l], k_head_norm_w[l])
        qnw, kvnw = q_norm_w[l][None, :], kv_norm_w[l][None, :]
        ps = pool_scale[l][None, :]
        ctx_out = l < DEPTH - 1

        p_c, uv_c = _in_proj(ctx, mod_ctx, nw, w_in_p, cs_tab, layer=l, per_batch_mod=False, tm=tm_ctx)
        q_c, k_c, v_c = _qkv(p_c, qnw, kvnw, wuq, wukv, head_w, *rope_ctx, layer=l, tm=tm_ctx)
        p_l, uv_l = _in_proj(x, mod_lat, nw, w_in_p, cs_tab, layer=l, per_batch_mod=True, tm=tm_lat)
        q_l, k_l, v_l = _qkv(p_l, qnw, kvnw, wuq, wukv, head_w, *rope_lat, layer=l, tm=tm_qkv)

        attn_l = _attention(shift_is_safe, q_l, k_c, v_c, k_l, v_l, tq=1024)
        r_l = _pos_dft(dft_lat, uv_l.reshape(batch, 2 * seq, POOL_WIDTH), tm=512)
        x_new = _out_proj(x, mod_lat, p_l, r_l, attn_l, pw, ps, fw, wo, layer=l, per_batch_mod=True, tm=tm_lat)
        if ctx_out:
            attn_c = _attention(shift_is_safe, q_c, k_c, v_c, tq=n_ctx)
            r_c = _pos_dft(dft_ctx, uv_c.reshape(batch, 2 * n_ctx, POOL_WIDTH), tm=n_ctx)
            ctx = _out_proj(ctx, mod_ctx, p_c, r_c, attn_c, pw, ps, fw, wo, layer=l, per_batch_mod=False,
                            tm=tm_ctx)
        x = x_new
    return x
```

```python
import functools
import math

import numpy as np
import jax
import jax.numpy as jnp
from jax import lax
from jax.experimental import pallas as pl
from jax.experimental.pallas import tpu as pltpu

D_MODEL = 2048
DEPTH = 2
GRID_W = 64
POOL_WIDTH = 512
POOL_WINDOWS = (2, 4, 8, 16)
GROUP_CH = 128
N_GROUPS = 4
MLA_HEADS = 8
QK_NOPE_DIM = 128
QK_ROPE_DIM = 64
QK_HEAD_DIM = QK_NOPE_DIM + QK_ROPE_DIM
V_HEAD_DIM = 128
MLA_WIDTH = MLA_HEADS * V_HEAD_DIM
LORA_RANK = 512
N_FREQ_PER_AXIS = QK_ROPE_DIM // 4
ROPE_THETA = 10000.0
EPS = 1e-6
SM_SCALE = QK_HEAD_DIM ** -0.5
LOG2_E = math.log2(math.e)
MAX_SAFE_SHIFT = 60.0

COL_POOL = 0
COL_POOL_GATE = 512
COL_FNET = 1024
COL_FNET_GATE = 1536
COL_CQ = 2048
COL_CKV = 2560
COL_MLA_GATE = 3072
COL_KROPE = 4096
D_IN_PACKED = 4224
HEAD_PAD = 256

LANES = 128
SUBLANES = 8
VMEM_LIMIT = 56 * 1024 * 1024
ROW_SUB = 256
QKV_SUB = 256
POOL_HALO = 2 * SUBLANES
BAND_K = ROW_SUB + LANES

F32 = jnp.float32
BF16 = jnp.bfloat16

_ROT_SRC = np.array([i + 16 if (i % 32) < 16 else i - 16 for i in range(QK_ROPE_DIM)])
_ROT_SIGN = np.array([-1.0 if (i % 32) < 16 else 1.0 for i in range(QK_ROPE_DIM)], np.float32)


def _silu(x):
    return x * jax.nn.sigmoid(x)


def _dot(a, b):
    return jnp.dot(a, b, preferred_element_type=F32)


def _params(*sem):
    return pltpu.CompilerParams(dimension_semantics=sem, vmem_limit_bytes=VMEM_LIMIT)


def _resident(shape, layer=None):
    nd = len(shape)
    if layer is None:
        return pl.BlockSpec(shape, lambda *_: (0,) * nd, pipeline_mode=pl.Buffered(1))
    return pl.BlockSpec((None,) + tuple(shape), lambda *_: (layer,) + (0,) * nd, pipeline_mode=pl.Buffered(1))


def _ada_kernel(c_ref, w_ref, b_ref, o_ref):
    s = _silu(c_ref[...]).astype(BF16)
    o_ref[0] = _dot(s, w_ref[0].astype(BF16)) + b_ref[0]


def _ada_modulation(cc, w_ada, b_ada):
    tn = 512
    n3 = 3 * D_MODEL
    return pl.pallas_call(
        _ada_kernel,
        out_shape=jax.ShapeDtypeStruct((DEPTH, SUBLANES, n3), F32),
        grid=(DEPTH, n3 // tn),
        in_specs=[
            pl.BlockSpec((SUBLANES, D_MODEL), lambda l, j: (0, 0)),
            pl.BlockSpec((1, D_MODEL, tn), lambda l, j: (l, 0, j)),
            pl.BlockSpec((1, 1, tn), lambda l, j: (l, 0, j)),
        ],
        out_specs=pl.BlockSpec((1, SUBLANES, tn), lambda l, j: (l, 0, j)),
        compiler_params=_params("arbitrary", "arbitrary"),
        name="ada_modulation",
    )(cc, w_ada, b_ada.reshape(DEPTH, 1, n3))


def _in_proj_kernel(x_ref, mod_ref, nw_ref, w_ref, cs_ref, p_ref, uv_ref):
    mod = mod_ref[0]
    shift = mod[:, :D_MODEL]
    scale1 = (1.0 + mod[:, D_MODEL:2 * D_MODEL]) * nw_ref[...]
    tm = x_ref.shape[1]
    for r0 in range(0, tm, ROW_SUB):
        rows = slice(r0, min(r0 + ROW_SUB, tm))
        x = x_ref[0, rows, :]
        r = lax.rsqrt(jnp.mean(x * x, axis=-1, keepdims=True) + EPS)
        h = (x * r * scale1 + shift).astype(BF16)
        chunk = 512
        for c0 in range(0, D_IN_PACKED, chunk):
            c1 = min(c0 + chunk, D_IN_PACKED)
            pc = _dot(h, w_ref[:, c0:c1])
            p_ref[0, rows, c0:c1] = pc
            if c0 == COL_FNET:
                g = pc.astype(BF16)
                for hd in range(N_GROUPS):
                    cols = slice(hd * GROUP_CH, (hd + 1) * GROUP_CH)
                    uv = _dot(g[:, cols], cs_ref[...])
                    uv_ref[0, 0, rows, cols] = uv[:, :GROUP_CH].astype(BF16)
                    uv_ref[0, 1, rows, cols] = uv[:, GROUP_CH:].astype(BF16)


def _in_proj(xin, mod, norm_w, w_packed, cs_tab, *, layer, per_batch_mod, tm):
    b, n, _ = xin.shape
    mod_idx = (lambda bi, i: (bi, 0, 0)) if per_batch_mod else (lambda bi, i: (0, 0, 0))
    return pl.pallas_call(
        _in_proj_kernel,
        out_shape=(jax.ShapeDtypeStruct((b, n, D_IN_PACKED), F32),
                   jax.ShapeDtypeStruct((b, 2, n, POOL_WIDTH), BF16)),
        grid=(b, n // tm),
        in_specs=[
            pl.BlockSpec((1, tm, D_MODEL), lambda bi, i: (bi, i, 0)),
            pl.BlockSpec((1, 1, 3 * D_MODEL), mod_idx),
            _resident((1, D_MODEL)),
            _resident((D_MODEL, D_IN_PACKED), layer),
            _resident((GROUP_CH, 2 * GROUP_CH)),
        ],
        out_specs=(pl.BlockSpec((1, tm, D_IN_PACKED), lambda bi, i: (bi, i, 0)),
                   pl.BlockSpec((1, 2, tm, POOL_WIDTH), lambda bi, i: (bi, 0, i, 0))),
        compiler_params=_params("arbitrary", "arbitrary"),
        name="in_proj",
    )(xin, mod, norm_w, w_packed, cs_tab)


def _rms(x, w):
    return x * lax.rsqrt(jnp.mean(x * x, axis=-1, keepdims=True) + EPS) * w


def _qkv_kernel(cq_ref, ckv_ref, kr_ref, qnw_ref, kvnw_ref, wuq_ref, wukv_ref, hw_ref,
                cos_ref, sin_ref, q_ref, k_ref, v_ref):
    hw = hw_ref[...]
    q_nope_w, q_rope_w, q_rot_w = hw[0:1], hw[1:2], hw[2:3]
    k_nope_w, k_rope_w, k_rot_w = hw[3:4], hw[4:5], hw[5:6]
    q_pad, k_pad = hw[6:7], hw[7:8]
    tm = cq_ref.shape[1]
    sub = min(QKV_SUB, tm)
    one_col = (lax.broadcasted_iota(jnp.int32, (sub, LANES), 1) == 0).astype(BF16)
    inv_dim = 1.0 / QK_HEAD_DIM

    def rope(slab, t_cos, t_sin):
        return slab * t_cos + pltpu.roll(slab, LANES // 2, 1) * t_sin

    for r0 in range(0, tm, sub):
        rows = slice(r0, r0 + sub)
        cos = cos_ref[rows, :]
        sin = sin_ref[rows, :]
        q_cos, q_sin = cos * q_rope_w, sin * q_rot_w
        cqn = _rms(cq_ref[0, rows, :], qnw_ref[...]).astype(BF16)
        for hd in range(MLA_HEADS):
            qh = _dot(cqn, wuq_ref[:, hd * HEAD_PAD:(hd + 1) * HEAD_PAD])
            qn, qs = qh[:, :LANES], qh[:, LANES:]
            ss = jnp.sum(qn * qn, axis=-1, keepdims=True) + 0.5 * jnp.sum(qs * qs, axis=-1, keepdims=True)
            r = lax.rsqrt(ss * inv_dim + EPS) * (SM_SCALE * LOG2_E)
            q_ref[0, rows, hd * HEAD_PAD:hd * HEAD_PAD + LANES] = (qn * r * q_nope_w).astype(BF16)
            q_ref[0, rows, hd * HEAD_PAD + LANES:(hd + 1) * HEAD_PAD] = (
                rope(qs, q_cos, q_sin) * r + q_pad).astype(BF16)

        ckvn = _rms(ckv_ref[0, rows, :], kvnw_ref[...]).astype(BF16)
        ks = kr_ref[0, rows, :]
        ss_rope = 0.5 * jnp.sum(ks * ks, axis=-1, keepdims=True)
        k_rope = rope(ks, cos * k_rope_w, sin * k_rot_w)
        for hd in range(MLA_HEADS):
            kn = _dot(ckvn, wukv_ref[:, hd * LANES:(hd + 1) * LANES])
            ss = jnp.sum(kn * kn, axis=-1, keepdims=True) + ss_rope
            r = lax.rsqrt(ss * inv_dim + EPS)
            k_ref[0, rows, hd * HEAD_PAD:hd * HEAD_PAD + LANES] = (kn * r * k_nope_w).astype(BF16)
            k_ref[0, rows, hd * HEAD_PAD + LANES:(hd + 1) * HEAD_PAD] = (k_rope * r + k_pad).astype(BF16)
        v = _dot(ckvn, wukv_ref[:, MLA_HEADS * LANES:]).astype(BF16)
        for hd in range(MLA_HEADS):
            v_ref[0, rows, hd * HEAD_PAD:hd * HEAD_PAD + LANES] = v[:, hd * LANES:(hd + 1) * LANES]
            v_ref[0, rows, hd * HEAD_PAD + LANES:(hd + 1) * HEAD_PAD] = one_col


def _qkv(p, q_norm_w, kv_norm_w, wuq, wukv, head_w, cos_tab, sin_tab, *, layer, tm):
    b, n, _ = p.shape
    row = lambda bi, i: (i, 0)
    return pl.pallas_call(
        _qkv_kernel,
        out_shape=(jax.ShapeDtypeStruct((b, n, MLA_HEADS * HEAD_PAD), BF16),
                   jax.ShapeDtypeStruct((b, n, MLA_HEADS * HEAD_PAD), BF16),
                   jax.ShapeDtypeStruct((b, n, MLA_HEADS * HEAD_PAD), BF16)),
        grid=(b, n // tm),
        in_specs=[
            pl.BlockSpec((1, tm, LORA_RANK), lambda bi, i: (bi, i, COL_CQ // LORA_RANK)),
            pl.BlockSpec((1, tm, LORA_RANK), lambda bi, i: (bi, i, COL_CKV // LORA_RANK)),
            pl.BlockSpec((1, tm, LANES), lambda bi, i: (bi, i, COL_KROPE // LANES)),
            _resident((1, LORA_RANK)),
            _resident((1, LORA_RANK)),
            _resident((LORA_RANK, MLA_HEADS * HEAD_PAD), layer),
            _resident((LORA_RANK, 2 * MLA_WIDTH), layer),
            _resident((SUBLANES, LANES)),
            pl.BlockSpec((tm, LANES), row),
            pl.BlockSpec((tm, LANES), row),
        ],
        out_specs=(pl.BlockSpec((1, tm, MLA_HEADS * HEAD_PAD), lambda bi, i: (bi, i, 0)),
                   pl.BlockSpec((1, tm, MLA_HEADS * HEAD_PAD), lambda bi, i: (bi, i, 0)),
                   pl.BlockSpec((1, tm, MLA_HEADS * HEAD_PAD), lambda bi, i: (bi, i, 0))),
        compiler_params=_params("arbitrary", "arbitrary"),
        name="qkv",
    )(p, p, p, q_norm_w, kv_norm_w, wuq, wukv, head_w, cos_tab, sin_tab)


def _scores(q, k):
    return lax.dot_general(q, k, (((1,), (1,)), ((), ())), preferred_element_type=F32)


def _attn_finish(acc, o_ref, rows=slice(None)):
    o_ref[0, rows, :] = (acc[:, :V_HEAD_DIM] / acc[:, V_HEAD_DIM:V_HEAD_DIM + 1]).astype(BF16)


def _attn_shifted_kernel(*refs, tk, n_lat):
    if n_lat:
        q_ref, kc_ref, vc_ref, kl_ref, vl_ref, o_ref = refs
    else:
        q_ref, kc_ref, vc_ref, o_ref = refs
    tq = q_ref.shape[1]
    chains = [slice(r0, r0 + tq // 2) for r0 in (0, tq // 2)] if tq >= 2 * ROW_SUB else [slice(0, tq)]
    qs = [q_ref[0, rows, :] for rows in chains]

    def tile(q, k, v):
        return _dot(jnp.exp2(_scores(q, k)).astype(BF16), v)

    accs = [tile(q, kc_ref[0], vc_ref[0]) for q in qs]
    for j in range(n_lat // tk):
        k, v = kl_ref[0, j * tk:(j + 1) * tk, :], vl_ref[0, j * tk:(j + 1) * tk, :]
        accs = [acc + tile(q, k, v) for q, acc in zip(qs, accs)]
    for rows, acc in zip(chains, accs):
        _attn_finish(acc, o_ref, rows)


def _attn_online_kernel(*refs, tk, n_lat):
    if n_lat:
        q_ref, kc_ref, vc_ref, kl_ref, vl_ref, o_ref = refs
    else:
        q_ref, kc_ref, vc_ref, o_ref = refs
    q = q_ref[0]
    tq = q.shape[0]

    def step(k, v, m, acc):
        s = _scores(q, k)
        m_new = jnp.maximum(m, jnp.max(s, axis=-1, keepdims=True))
        p = jnp.exp2(s - m_new)
        return m_new, jnp.exp2(m - m_new) * acc + _dot(p.astype(BF16), v)

    m = jnp.full((tq, 1), -jnp.inf, F32)
    acc = jnp.zeros((tq, HEAD_PAD), F32)
    m, acc = step(kc_ref[0], vc_ref[0], m, acc)
    if n_lat:
        def body(j, carry):
            rows = pl.ds(pl.multiple_of(j * tk, tk), tk)
            return step(kl_ref[0, rows, :], vl_ref[0, rows, :], *carry)
        m, acc = lax.fori_loop(0, n_lat // tk, body, (m, acc))
    _attn_finish(acc, o_ref)


def _attention(shift_is_safe, q, k_ctx, v_ctx, k_lat=None, v_lat=None, *, tq, tk=512):
    b, nq, _ = q.shape
    n_ctx = k_ctx.shape[1]
    n_lat = 0 if k_lat is None else k_lat.shape[1]
    head = lambda bi, hd, i: (bi, 0, hd)
    in_specs = [
        pl.BlockSpec((1, tq, HEAD_PAD), lambda bi, hd, i: (bi, i, hd)),
        pl.BlockSpec((1, n_ctx, HEAD_PAD), head),
        pl.BlockSpec((1, n_ctx, HEAD_PAD), head),
    ]
    args = [q, k_ctx, v_ctx]
    if n_lat:
        in_specs += [pl.BlockSpec((1, n_lat, HEAD_PAD), head)] * 2
        args += [k_lat, v_lat]

    def run(body, name):
        return pl.pallas_call(
            functools.partial(body, tk=tk, n_lat=n_lat),
            out_shape=jax.ShapeDtypeStruct((b, nq, MLA_WIDTH), BF16),
            grid=(b, MLA_HEADS, nq // tq),
            in_specs=in_specs,
            out_specs=pl.BlockSpec((1, tq, V_HEAD_DIM), lambda bi, hd, i: (bi, i, hd)),
            compiler_params=_params("arbitrary", "arbitrary", "arbitrary"),
            name=name,
        )

    return lax.cond(shift_is_safe,
                    lambda *a: run(_attn_shifted_kernel, "attention_shifted")(*a),
                    lambda *a: run(_attn_online_kernel, "attention_online")(*a),
                    *args)


def _pos_dft_kernel(tab_ref, lo_ref, rev_ref, mid_ref, dm_ref, dp_ref, dmid_ref, ue_ref, vo_ref, *, n):
    i = pl.program_id(1)
    half = n // 2
    tm = tab_ref.shape[0]
    inv = 1.0 / math.sqrt(n)
    u_mid = mid_ref[0].astype(F32) * inv

    @pl.when(i == 0)
    def _():
        ue = lo_ref[0, 0].astype(F32) + rev_ref[0, 0].astype(F32)
        ue_ref[...] = ue.astype(BF16)
        vo_ref[...] = (lo_ref[0, 1].astype(F32) - rev_ref[0, 1].astype(F32)).astype(BF16)
        j = lax.broadcasted_iota(jnp.int32, (SUBLANES, half), 1)
        row0 = lax.broadcasted_iota(jnp.int32, (SUBLANES, half), 0) == 0
        alt = jnp.where(row0, (1 - 2 * (j % 2)).astype(F32) * inv, 0.0).astype(BF16)
        dmid_ref[0] = (_dot(alt, ue_ref[...]) + u_mid).astype(BF16)

    k = i * tm + lax.broadcasted_iota(jnp.int32, (tm, 1), 0)
    p = _dot(tab_ref[:, :half], ue_ref[...]) + (1 - 2 * (k % 2)).astype(F32) * u_mid
    q = _dot(tab_ref[:, half:], vo_ref[...])
    dm_ref[0] = (p - q).astype(BF16)
    dp_ref[0] = (p + q).astype(BF16)


def _pos_dft(tab, uv, *, tm):
    b, _, n, _ = uv.shape
    half = n // 2
    rev = jnp.concatenate([jnp.zeros((b, 2, 1, POOL_WIDTH), BF16), jnp.flip(uv[:, :, half + 1:], axis=2)], axis=2)
    mid = uv[:, 0, half][:, None, :]
    dm, dp, dmid = pl.pallas_call(
        functools.partial(_pos_dft_kernel, n=n),
        out_shape=(jax.ShapeDtypeStruct((b, half, POOL_WIDTH), BF16),
                   jax.ShapeDtypeStruct((b, half, POOL_WIDTH), BF16),
                   jax.ShapeDtypeStruct((b, SUBLANES, POOL_WIDTH), BF16)),
        grid=(b, half // tm),
        in_specs=[pl.BlockSpec((tm, n), lambda bi, i: (i, 0)),
                  pl.BlockSpec((1, 2, half, POOL_WIDTH), lambda bi, i: (bi, 0, 0, 0)),
                  pl.BlockSpec((1, 2, half, POOL_WIDTH), lambda bi, i: (bi, 0, 0, 0)),
                  pl.BlockSpec((1, 1, POOL_WIDTH), lambda bi, i: (bi, 0, 0))],
        out_specs=(pl.BlockSpec((1, tm, POOL_WIDTH), lambda bi, i: (bi, i, 0)),
                   pl.BlockSpec((1, tm, POOL_WIDTH), lambda bi, i: (bi, i, 0)),
                   pl.BlockSpec((1, SUBLANES, POOL_WIDTH), lambda bi, i: (bi, 0, 0))),
        scratch_shapes=[pltpu.VMEM((half, POOL_WIDTH), BF16), pltpu.VMEM((half, POOL_WIDTH), BF16)],
        compiler_params=_params("arbitrary", "arbitrary"),
        name="pos_dft",
    )(tab, uv, rev, mid)
    return jnp.concatenate([dm, dmid[:, :1], jnp.flip(dp[:, 1:], axis=1)], axis=1)


def _out_proj_kernel(x_ref, mod_ref, pu_ref, prev_ref, next_ref, pg_ref, fg_ref, mg_ref, r_ref, at_ref,
                     band_ref, pw_ref, ps_ref, fw_ref, wo_ref, o_ref, ext_ref, *, tm, n):
    i = pl.program_id(1)
    ext_ref[0:POOL_HALO] = jnp.where(i > 0, prev_ref[0], 0.0).astype(BF16)
    ext_ref[POOL_HALO:POOL_HALO + tm] = pu_ref[0].astype(BF16)
    ext_ref[POOL_HALO + tm:2 * POOL_HALO + tm] = jnp.where(i < pl.num_programs(1) - 1, next_ref[0], 0.0).astype(BF16)
    ext_ref[2 * POOL_HALO + tm:] = jnp.zeros((BAND_K - ROW_SUB - 2 * POOL_HALO, POOL_WIDTH), BF16)
    gate = mod_ref[0][:, 2 * D_MODEL:]
    pair = 2 * GROUP_CH
    for r0 in range(0, tm, ROW_SUB):
        rows = slice(r0, r0 + ROW_SUB)
        t = i * tm + r0 + lax.broadcasted_iota(jnp.int32, (ROW_SUB, 1), 0)
        pooled = []
        for g, w in enumerate(POOL_WINDOWS):
            cols = slice(g * GROUP_CH, (g + 1) * GROUP_CH)
            lo = w // 2
            hi = w - lo - 1
            s = _dot(band_ref[g], ext_ref[r0:r0 + BAND_K, cols])
            cnt = (jnp.minimum(t + hi, n - 1) - jnp.maximum(t - lo, 0) + 1).astype(F32)
            pooled.append((s / cnt - pu_ref[0, rows, cols]).astype(BF16))
        parts = []
        for j in range(N_GROUPS // 2):
            cols = slice(j * pair, (j + 1) * pair)
            y = _dot(jnp.concatenate(pooled[2 * j:2 * j + 2], axis=-1), pw_ref[j]) * ps_ref[:, cols]
            parts.append((_silu(pg_ref[0, rows, cols]) * y).astype(BF16))
        for j in range(N_GROUPS // 2):
            cols = slice(j * pair, (j + 1) * pair)
            f = _dot(r_ref[0, rows, cols], fw_ref[j])
            parts.append((_silu(fg_ref[0, rows, cols]) * f).astype(BF16))
        parts.append((_silu(mg_ref[0, rows, :]) * at_ref[0, rows, :].astype(F32)).astype(BF16))
        mixed = jnp.concatenate(parts, axis=-1)
        chunk = 512
        for c0 in range(0, D_MODEL, chunk):
            y = _dot(mixed, wo_ref[:, c0:c0 + chunk])
            o_ref[0, rows, c0:c0 + chunk] = x_ref[0, rows, c0:c0 + chunk] + gate[:, c0:c0 + chunk] * y


def _out_proj(xin, mod, p, r, attn, pool_w, pool_scale, fnet_w, w_out, *, layer, per_batch_mod, tm):
    b, n, _ = xin.shape
    mod_idx = (lambda bi, i: (bi, 0, 0)) if per_batch_mod else (lambda bi, i: (0, 0, 0))
    hb = tm // POOL_HALO
    last = n // POOL_HALO - 1
    pcol = lambda width, off: pl.BlockSpec((1, tm, width), lambda bi, i: (bi, i, off // width))
    pair = 2 * GROUP_CH
    return pl.pallas_call(
        functools.partial(_out_proj_kernel, tm=tm, n=n),
        out_shape=jax.ShapeDtypeStruct((b, n, D_MODEL), F32),
        grid=(b, n // tm),
        in_specs=[
            pl.BlockSpec((1, tm, D_MODEL), lambda bi, i: (bi, i, 0)),
            pl.BlockSpec((1, 1, 3 * D_MODEL), mod_idx),
            pcol(POOL_WIDTH, COL_POOL),
            pl.BlockSpec((1, POOL_HALO, POOL_WIDTH), lambda bi, i: (bi, jnp.maximum(i * hb - 1, 0), 0)),
            pl.BlockSpec((1, POOL_HALO, POOL_WIDTH), lambda bi, i: (bi, jnp.minimum((i + 1) * hb, last), 0)),
            pcol(POOL_WIDTH, COL_POOL_GATE),
            pcol(POOL_WIDTH, COL_FNET_GATE),
            pcol(MLA_WIDTH, COL_MLA_GATE),
            pl.BlockSpec((1, tm, POOL_WIDTH), lambda bi, i: (bi, i, 0)),
            pl.BlockSpec((1, tm, MLA_WIDTH), lambda bi, i: (bi, i, 0)),
            _resident((N_GROUPS, ROW_SUB, BAND_K)),
            _resident((N_GROUPS // 2, pair, pair), layer),
            _resident((1, POOL_WIDTH)),
            _resident((N_GROUPS // 2, pair, pair), layer),
            _resident((D_MODEL, D_MODEL), layer),
        ],
        out_specs=pl.BlockSpec((1, tm, D_MODEL), lambda bi, i: (bi, i, 0)),
        scratch_shapes=[pltpu.VMEM((tm + BAND_K - ROW_SUB, POOL_WIDTH), BF16)],
        compiler_params=_params("arbitrary", "arbitrary"),
        name="out_proj",
    )(xin, mod, p, p, p, p, p, p, r, attn, _pool_band_table(), pool_w, pool_scale, fnet_w, w_out)


def _rope_tables(n):
    t = jnp.arange(n, dtype=jnp.int32)
    inv_freq = ROPE_THETA ** (-jnp.arange(N_FREQ_PER_AXIS, dtype=F32) / N_FREQ_PER_AXIS)
    ang_r = (t // GRID_W).astype(F32)[:, None] * inv_freq
    ang_c = (t % GRID_W).astype(F32)[:, None] * inv_freq
    ang = jnp.concatenate([ang_r, ang_r, ang_c, ang_c], axis=-1)
    pad = jnp.zeros((n, LANES - QK_ROPE_DIM), F32)
    return (jnp.concatenate([jnp.cos(ang), pad], axis=-1),
            jnp.concatenate([jnp.sin(ang) * _ROT_SIGN, pad], axis=-1))


def _identity_rope_tables(n):
    pad = jnp.zeros((n, LANES - QK_ROPE_DIM), F32)
    return jnp.concatenate([jnp.ones((n, QK_ROPE_DIM), F32), pad], axis=-1), jnp.zeros((n, LANES), F32)


def _channel_dft_table():
    k = np.arange(GROUP_CH)
    ang = 2.0 * np.pi * ((k[:, None] * k[None, :]) % GROUP_CH) / GROUP_CH
    tab = np.concatenate([np.cos(ang), np.sin(ang)], axis=1) / math.sqrt(GROUP_CH)
    return jnp.asarray(tab, F32).astype(BF16)


def _position_dft_table(n):
    half = n // 2
    g = 1
    while g * g < n:
        g *= 2
    period = n // g
    j = jnp.arange(half, dtype=jnp.int32)
    a = jnp.arange(half // g, dtype=jnp.int32)
    bb = jnp.arange(g, dtype=jnp.int32)
    ang_a = ((a[:, None] * j[None, :]) % period).astype(F32) * (2.0 * math.pi / period)
    ang_b = ((bb[:, None] * j[None, :]) % n).astype(F32) * (2.0 * math.pi / n)
    ca, sa = jnp.cos(ang_a)[:, None, :], jnp.sin(ang_a)[:, None, :]
    cb, sb = jnp.cos(ang_b)[None, :, :], jnp.sin(ang_b)[None, :, :]
    scale = 1.0 / math.sqrt(n)
    cos = ((ca * cb - sa * sb) * scale).reshape(half, half)
    sin = ((sa * cb + ca * sb) * scale).reshape(half, half)
    return jnp.concatenate([cos, sin], axis=1).astype(BF16)


def _pool_band_table():
    t = np.arange(ROW_SUB)[:, None]
    d = np.arange(BAND_K)[None, :] - POOL_HALO - t
    bands = [((d >= -(w // 2)) & (d <= w - w // 2 - 1)).astype(np.float32) for w in POOL_WINDOWS]
    return jnp.asarray(np.stack(bands), BF16)


def _pair_block_diag(w):
    even, odd = w[:, 0::2], w[:, 1::2]
    zero = jnp.zeros_like(even)
    return jnp.concatenate([jnp.concatenate([even, zero], axis=-1),
                            jnp.concatenate([zero, odd], axis=-1)], axis=-2).astype(BF16)


def _pack_w_in(w):
    kr = w[..., 3072:3136]
    return jnp.concatenate([w[..., :3072], w[..., 3136:], kr, kr[..., _ROT_SRC]], axis=-1).astype(BF16)


def _pack_w_uq(w):
    w = w.reshape(DEPTH, LORA_RANK, MLA_HEADS, QK_HEAD_DIM)
    rope = w[..., QK_NOPE_DIM:]
    return jnp.concatenate([w[..., :QK_NOPE_DIM], rope, rope[..., _ROT_SRC]], axis=-1).reshape(
        DEPTH, LORA_RANK, MLA_HEADS * HEAD_PAD).astype(BF16)


def _pack_w_ukv(w):
    w = w.reshape(DEPTH, LORA_RANK, MLA_HEADS, QK_NOPE_DIM + V_HEAD_DIM)
    return jnp.concatenate([w[..., :QK_NOPE_DIM].reshape(DEPTH, LORA_RANK, -1),
                            w[..., QK_NOPE_DIM:].reshape(DEPTH, LORA_RANK, -1)], axis=-1).astype(BF16)


def _pack_head_w(qw, kw):
    pad = jnp.zeros((LANES - QK_ROPE_DIM,), F32)
    rows = []
    for w in (qw, kw):
        rope = w[QK_NOPE_DIM:]
        rows += [w[:QK_NOPE_DIM], jnp.concatenate([rope, pad]), jnp.concatenate([rope[_ROT_SRC], pad])]
    shift = QK_HEAD_DIM * SM_SCALE * LOG2_E * jnp.max(jnp.abs(qw)) * jnp.max(jnp.abs(kw))
    spare = (np.arange(LANES) == QK_ROPE_DIM).astype(np.float32)
    rows += [-shift * spare, jnp.asarray(spare)]
    return jnp.stack(rows), shift <= MAX_SAFE_SHIFT


def kernel(x, c, ctx, c_ctx, norm_w, w_ada, b_ada, w_in, pool_w, pool_scale, fnet_w, q_norm_w, w_uq,
           kv_norm_w, w_ukv, q_head_norm_w, k_head_norm_w, w_out):
    batch, seq, _ = x.shape
    n_ctx = ctx.shape[1]
    cc = jnp.concatenate([c, c_ctx[None, :], jnp.zeros((SUBLANES - batch - 1, D_MODEL), F32)], axis=0)
    mod_all = _ada_modulation(cc, w_ada, b_ada)

    cs_tab = _channel_dft_table()
    rope_lat = _rope_tables(seq)
    rope_ctx = _identity_rope_tables(n_ctx)
    dft_lat = _position_dft_table(seq)
    dft_ctx = _position_dft_table(n_ctx)
    tm_lat, tm_qkv, tm_ctx = 512, 512, n_ctx

    w_in_p, wuq, wukv = _pack_w_in(w_in), _pack_w_uq(w_uq), _pack_w_ukv(w_ukv)
    pw, fw, wo = _pair_block_diag(pool_w), _pair_block_diag(fnet_w), w_out.astype(BF16)

    for l in range(DEPTH):
        mod_lat = mod_all[l, :batch][:, None, :]
        mod_ctx = mod_all[l, batch:batch + 1][:, None, :]
        nw = norm_w[l][None, :]
        head_w, shift_is_safe = _pack_head_w(q_head_norm_w[l], k_head_norm_w[l])
        qnw, kvnw = q_norm_w[l][None, :], kv_norm_w[l][None, :]
        ps = pool_scale[l][None, :]
        ctx_out = l < DEPTH - 1

        p_c, uv_c = _in_proj(ctx, mod_ctx, nw, w_in_p, cs_tab, layer=l, per_batch_mod=False, tm=tm_ctx)
        q_c, k_c, v_c = _qkv(p_c, qnw, kvnw, wuq, wukv, head_w, *rope_ctx, layer=l, tm=tm_ctx)
        p_l, uv_l = _in_proj(x, mod_lat, nw, w_in_p, cs_tab, layer=l, per_batch_mod=True, tm=tm_lat)
        q_l, k_l, v_l = _qkv(p_l, qnw, kvnw, wuq, wukv, head_w, *rope_lat, layer=l, tm=tm_qkv)

        attn_l = _attention(shift_is_safe, q_l, k_c, v_c, k_l, v_l, tq=2048)
        r_l = _pos_dft(dft_lat, uv_l, tm=512)
        x_new = _out_proj(x, mod_lat, p_l, r_l, attn_l, pw, ps, fw, wo, layer=l, per_batch_mod=True, tm=tm_lat)
        if ctx_out:
            attn_c = _attention(shift_is_safe, q_c, k_c, v_c, tq=n_ctx)
            r_c = _pos_dft(dft_ctx, uv_c, tm=n_ctx // 2)
            ctx = _out_proj(ctx, mod_ctx, p_c, r_c, attn_c, pw, ps, fw, wo, layer=l, per_batch_mod=False,
                            tm=tm_ctx)
        x = x_new
    return x
```

```python
import functools
import math

import numpy as np
import jax
import jax.numpy as jnp
from jax import lax
from jax.experimental import pallas as pl
from jax.experimental.pallas import tpu as pltpu

D_MODEL = 2048
DEPTH = 2
GRID_W = 64
POOL_WIDTH = 512
POOL_WINDOWS = (2, 4, 8, 16)
GROUP_CH = 128
N_GROUPS = 4
MLA_HEADS = 8
QK_NOPE_DIM = 128
QK_ROPE_DIM = 64
QK_HEAD_DIM = QK_NOPE_DIM + QK_ROPE_DIM
V_HEAD_DIM = 128
MLA_WIDTH = MLA_HEADS * V_HEAD_DIM
LORA_RANK = 512
N_FREQ_PER_AXIS = QK_ROPE_DIM // 4
ROPE_THETA = 10000.0
EPS = 1e-6
SM_SCALE = QK_HEAD_DIM ** -0.5
LOG2_E = math.log2(math.e)
MAX_SAFE_SHIFT = 60.0

COL_POOL = 0
COL_POOL_GATE = 512
COL_FNET = 1024
COL_FNET_GATE = 1536
COL_CQ = 2048
COL_CKV = 2560
COL_MLA_GATE = 3072
COL_KROPE = 4096
D_IN_PACKED = 4224
HEAD_PAD = 256

LANES = 128
SUBLANES = 8
VMEM_LIMIT = 56 * 1024 * 1024
ROW_SUB = 256
QKV_SUB = 256
POOL_HALO = 2 * SUBLANES
BAND_K = ROW_SUB + LANES

F32 = jnp.float32
BF16 = jnp.bfloat16

_ROT_SRC = np.array([i + 16 if (i % 32) < 16 else i - 16 for i in range(QK_ROPE_DIM)])
_ROT_SIGN = np.array([-1.0 if (i % 32) < 16 else 1.0 for i in range(QK_ROPE_DIM)], np.float32)


def _silu(x):
    return x * jax.nn.sigmoid(x)


def _dot(a, b):
    return jnp.dot(a, b, preferred_element_type=F32)


def _params(*sem):
    return pltpu.CompilerParams(dimension_semantics=sem, vmem_limit_bytes=VMEM_LIMIT)


def _resident(shape, layer=None):
    nd = len(shape)
    if layer is None:
        return pl.BlockSpec(shape, lambda *_: (0,) * nd, pipeline_mode=pl.Buffered(1))
    return pl.BlockSpec((None,) + tuple(shape), lambda *_: (layer,) + (0,) * nd, pipeline_mode=pl.Buffered(1))


def _ada_kernel(c_ref, w_ref, b_ref, o_ref):
    s = _silu(c_ref[...]).astype(BF16)
    o_ref[0] = _dot(s, w_ref[0].astype(BF16)) + b_ref[0]


def _ada_modulation(cc, w_ada, b_ada):
    tn = 512
    n3 = 3 * D_MODEL
    return pl.pallas_call(
        _ada_kernel,
        out_shape=jax.ShapeDtypeStruct((DEPTH, SUBLANES, n3), F32),
        grid=(DEPTH, n3 // tn),
        in_specs=[
            pl.BlockSpec((SUBLANES, D_MODEL), lambda l, j: (0, 0)),
            pl.BlockSpec((1, D_MODEL, tn), lambda l, j: (l, 0, j)),
            pl.BlockSpec((1, 1, tn), lambda l, j: (l, 0, j)),
        ],
        out_specs=pl.BlockSpec((1, SUBLANES, tn), lambda l, j: (l, 0, j)),
        compiler_params=_params("arbitrary", "arbitrary"),
        name="ada_modulation",
    )(cc, w_ada, b_ada.reshape(DEPTH, 1, n3))


def _in_proj_kernel(x_ref, mod_ref, nw_ref, w_ref, cs_ref, p_ref, uv_ref):
    mod = mod_ref[0]
    shift = mod[:, :D_MODEL]
    scale1 = (1.0 + mod[:, D_MODEL:2 * D_MODEL]) * nw_ref[...]
    tm = x_ref.shape[1]
    for r0 in range(0, tm, ROW_SUB):
        rows = slice(r0, min(r0 + ROW_SUB, tm))
        x = x_ref[0, rows, :]
        r = lax.rsqrt(jnp.mean(x * x, axis=-1, keepdims=True) + EPS)
        h = (x * r * scale1 + shift).astype(BF16)
        chunk = 512
        for c0 in range(0, D_IN_PACKED, chunk):
            c1 = min(c0 + chunk, D_IN_PACKED)
            pc = _dot(h, w_ref[:, c0:c1])
            p_ref[0, rows, c0:c1] = pc
            if c0 == COL_FNET:
                g = pc.astype(BF16)
                for hd in range(N_GROUPS):
                    cols = slice(hd * GROUP_CH, (hd + 1) * GROUP_CH)
                    uv = _dot(g[:, cols], cs_ref[...])
                    uv_ref[0, 0, rows, cols] = uv[:, :GROUP_CH].astype(BF16)
                    uv_ref[0, 1, rows, cols] = uv[:, GROUP_CH:].astype(BF16)


def _in_proj(xin, mod, norm_w, w_packed, cs_tab, *, layer, per_batch_mod, tm):
    b, n, _ = xin.shape
    mod_idx = (lambda bi, i: (bi, 0, 0)) if per_batch_mod else (lambda bi, i: (0, 0, 0))
    return pl.pallas_call(
        _in_proj_kernel,
        out_shape=(jax.ShapeDtypeStruct((b, n, D_IN_PACKED), F32),
                   jax.ShapeDtypeStruct((b, 2, n, POOL_WIDTH), BF16)),
        grid=(b, n // tm),
        in_specs=[
            pl.BlockSpec((1, tm, D_MODEL), lambda bi, i: (bi, i, 0)),
            pl.BlockSpec((1, 1, 3 * D_MODEL), mod_idx),
            _resident((1, D_MODEL)),
            _resident((D_MODEL, D_IN_PACKED), layer),
            _resident((GROUP_CH, 2 * GROUP_CH)),
        ],
        out_specs=(pl.BlockSpec((1, tm, D_IN_PACKED), lambda bi, i: (bi, i, 0)),
                   pl.BlockSpec((1, 2, tm, POOL_WIDTH), lambda bi, i: (bi, 0, i, 0))),
        compiler_params=_params("arbitrary", "arbitrary"),
        name="in_proj",
    )(xin, mod, norm_w, w_packed, cs_tab)


def _rms(x, w):
    return x * lax.rsqrt(jnp.mean(x * x, axis=-1, keepdims=True) + EPS) * w


def _qkv_kernel(cq_ref, ckv_ref, kr_ref, qnw_ref, kvnw_ref, wuq_ref, wukv_ref, hw_ref,
                cos_ref, sin_ref, q_ref, k_ref, v_ref):
    hw = hw_ref[...]
    q_nope_w, q_rope_w, q_rot_w = hw[0:1], hw[1:2], hw[2:3]
    k_nope_w, k_rope_w, k_rot_w = hw[3:4], hw[4:5], hw[5:6]
    q_pad, k_pad = hw[6:7], hw[7:8]
    tm = cq_ref.shape[1]
    sub = min(QKV_SUB, tm)
    one_col = (lax.broadcasted_iota(jnp.int32, (sub, LANES), 1) == 0).astype(BF16)
    inv_dim = 1.0 / QK_HEAD_DIM

    def rope(slab, t_cos, t_sin):
        return slab * t_cos + pltpu.roll(slab, LANES // 2, 1) * t_sin

    for r0 in range(0, tm, sub):
        rows = slice(r0, r0 + sub)
        cos = cos_ref[rows, :]
        sin = sin_ref[rows, :]
        q_cos, q_sin = cos * q_rope_w, sin * q_rot_w
        cqn = _rms(cq_ref[0, rows, :], qnw_ref[...]).astype(BF16)
        for hd in range(MLA_HEADS):
            qh = _dot(cqn, wuq_ref[:, hd * HEAD_PAD:(hd + 1) * HEAD_PAD])
            qn, qs = qh[:, :LANES], qh[:, LANES:]
            ss = jnp.sum(qn * qn, axis=-1, keepdims=True) + 0.5 * jnp.sum(qs * qs, axis=-1, keepdims=True)
            r = lax.rsqrt(ss * inv_dim + EPS) * (SM_SCALE * LOG2_E)
            q_ref[0, rows, hd * HEAD_PAD:hd * HEAD_PAD + LANES] = (qn * r * q_nope_w).astype(BF16)
            q_ref[0, rows, hd * HEAD_PAD + LANES:(hd + 1) * HEAD_PAD] = (
                rope(qs, q_cos, q_sin) * r + q_pad).astype(BF16)

        ckvn = _rms(ckv_ref[0, rows, :], kvnw_ref[...]).astype(BF16)
        ks = kr_ref[0, rows, :]
        ss_rope = 0.5 * jnp.sum(ks * ks, axis=-1, keepdims=True)
        k_rope = rope(ks, cos * k_rope_w, sin * k_rot_w)
        for hd in range(MLA_HEADS):
            kn = _dot(ckvn, wukv_ref[:, hd * LANES:(hd + 1) * LANES])
            ss = jnp.sum(kn * kn, axis=-1, keepdims=True) + ss_rope
            r = lax.rsqrt(ss * inv_dim + EPS)
            k_ref[0, rows, hd * HEAD_PAD:hd * HEAD_PAD + LANES] = (kn * r * k_nope_w).astype(BF16)
            k_ref[0, rows, hd * HEAD_PAD + LANES:(hd + 1) * HEAD_PAD] = (k_rope * r + k_pad).astype(BF16)
        v = _dot(ckvn, wukv_ref[:, MLA_HEADS * LANES:]).astype(BF16)
        for hd in range(MLA_HEADS):
            v_ref[0, rows, hd * HEAD_PAD:hd * HEAD_PAD + LANES] = v[:, hd * LANES:(hd + 1) * LANES]
            v_ref[0, rows, hd * HEAD_PAD + LANES:(hd + 1) * HEAD_PAD] = one_col


def _qkv(p, q_norm_w, kv_norm_w, wuq, wukv, head_w, cos_tab, sin_tab, *, layer, tm):
    b, n, _ = p.shape
    row = lambda bi, i: (i, 0)
    return pl.pallas_call(
        _qkv_kernel,
        out_shape=(jax.ShapeDtypeStruct((b, n, MLA_HEADS * HEAD_PAD), BF16),
                   jax.ShapeDtypeStruct((b, n, MLA_HEADS * HEAD_PAD), BF16),
                   jax.ShapeDtypeStruct((b, n, MLA_HEADS * HEAD_PAD), BF16)),
        grid=(b, n // tm),
        in_specs=[
            pl.BlockSpec((1, tm, LORA_RANK), lambda bi, i: (bi, i, COL_CQ // LORA_RANK)),
            pl.BlockSpec((1, tm, LORA_RANK), lambda bi, i: (bi, i, COL_CKV // LORA_RANK)),
            pl.BlockSpec((1, tm, LANES), lambda bi, i: (bi, i, COL_KROPE // LANES)),
            _resident((1, LORA_RANK)),
            _resident((1, LORA_RANK)),
            _resident((LORA_RANK, MLA_HEADS * HEAD_PAD), layer),
            _resident((LORA_RANK, 2 * MLA_WIDTH), layer),
            _resident((SUBLANES, LANES)),
            pl.BlockSpec((tm, LANES), row),
            pl.BlockSpec((tm, LANES), row),
        ],
        out_specs=(pl.BlockSpec((1, tm, MLA_HEADS * HEAD_PAD), lambda bi, i: (bi, i, 0)),
                   pl.BlockSpec((1, tm, MLA_HEADS * HEAD_PAD), lambda bi, i: (bi, i, 0)),
                   pl.BlockSpec((1, tm, MLA_HEADS * HEAD_PAD), lambda bi, i: (bi, i, 0))),
        compiler_params=_params("arbitrary", "arbitrary"),
        name="qkv",
    )(p, p, p, q_norm_w, kv_norm_w, wuq, wukv, head_w, cos_tab, sin_tab)


def _scores(q, k):
    return lax.dot_general(q, k, (((1,), (1,)), ((), ())), preferred_element_type=F32)


def _attn_finish(acc, o_ref, rows=slice(None)):
    o_ref[0, rows, :] = (acc[:, :V_HEAD_DIM] / acc[:, V_HEAD_DIM:V_HEAD_DIM + 1]).astype(BF16)


def _attn_shifted_kernel(*refs, tk, n_lat):
    if n_lat:
        q_ref, kc_ref, vc_ref, kl_ref, vl_ref, o_ref = refs
    else:
        q_ref, kc_ref, vc_ref, o_ref = refs
    tq = q_ref.shape[1]
    chains = [slice(r0, r0 + tq // 2) for r0 in (0, tq // 2)] if tq >= 2 * ROW_SUB else [slice(0, tq)]
    qs = [q_ref[0, rows, :] for rows in chains]

    def tile(q, k, v):
        return _dot(jnp.exp2(_scores(q, k)).astype(BF16), v)

    accs = [tile(q, kc_ref[0], vc_ref[0]) for q in qs]
    for j in range(n_lat // tk):
        k, v = kl_ref[0, j * tk:(j + 1) * tk, :], vl_ref[0, j * tk:(j + 1) * tk, :]
        accs = [acc + tile(q, k, v) for q, acc in zip(qs, accs)]
    for rows, acc in zip(chains, accs):
        _attn_finish(acc, o_ref, rows)


def _attn_online_kernel(*refs, tk, n_lat):
    if n_lat:
        q_ref, kc_ref, vc_ref, kl_ref, vl_ref, o_ref = refs
    else:
        q_ref, kc_ref, vc_ref, o_ref = refs
    q = q_ref[0]
    tq = q.shape[0]

    def step(k, v, m, acc):
        s = _scores(q, k)
        m_new = jnp.maximum(m, jnp.max(s, axis=-1, keepdims=True))
        p = jnp.exp2(s - m_new)
        return m_new, jnp.exp2(m - m_new) * acc + _dot(p.astype(BF16), v)

    m = jnp.full((tq, 1), -jnp.inf, F32)
    acc = jnp.zeros((tq, HEAD_PAD), F32)
    m, acc = step(kc_ref[0], vc_ref[0], m, acc)
    if n_lat:
        def body(j, carry):
            rows = pl.ds(pl.multiple_of(j * tk, tk), tk)
            return step(kl_ref[0, rows, :], vl_ref[0, rows, :], *carry)
        m, acc = lax.fori_loop(0, n_lat // tk, body, (m, acc))
    _attn_finish(acc, o_ref)


def _attention(shift_is_safe, q, k_ctx, v_ctx, k_lat=None, v_lat=None, *, tq, tk=512):
    b, nq, _ = q.shape
    n_ctx = k_ctx.shape[1]
    n_lat = 0 if k_lat is None else k_lat.shape[1]
    head = lambda bi, hd, i: (bi, 0, hd)
    in_specs = [
        pl.BlockSpec((1, tq, HEAD_PAD), lambda bi, hd, i: (bi, i, hd)),
        pl.BlockSpec((1, n_ctx, HEAD_PAD), head),
        pl.BlockSpec((1, n_ctx, HEAD_PAD), head),
    ]
    args = [q, k_ctx, v_ctx]
    if n_lat:
        in_specs += [pl.BlockSpec((1, n_lat, HEAD_PAD), head)] * 2
        args += [k_lat, v_lat]

    def run(body, name):
        return pl.pallas_call(
            functools.partial(body, tk=tk, n_lat=n_lat),
            out_shape=jax.ShapeDtypeStruct((b, nq, MLA_WIDTH), BF16),
            grid=(b, MLA_HEADS, nq // tq),
            in_specs=in_specs,
            out_specs=pl.BlockSpec((1, tq, V_HEAD_DIM), lambda bi, hd, i: (bi, i, hd)),
            compiler_params=_params("arbitrary", "arbitrary", "arbitrary"),
            name=name,
        )

    return lax.cond(shift_is_safe,
                    lambda *a: run(_attn_shifted_kernel, "attention_shifted")(*a),
                    lambda *a: run(_attn_online_kernel, "attention_online")(*a),
                    *args)


def _pos_dft_kernel(tab_ref, uv_ref, o_ref, ue_ref, vo_ref, dp_ref, *, n):
    half = n // 2
    tr = min(ROW_SUB, half)
    tk = min(2 * ROW_SUB, half)
    inv = 1.0 / math.sqrt(n)
    u_mid = uv_ref[0, 0, half:half + 1, :].astype(F32) * inv
    anti = (lax.broadcasted_iota(jnp.int32, (tr, tr), 0) + lax.broadcasted_iota(jnp.int32, (tr, tr), 1)
            == tr).astype(BF16)
    first = lax.broadcasted_iota(jnp.int32, (tr, 1), 0) == 0

    def reverse_after(tile, head_row):
        return jnp.where(first, head_row.astype(F32), _dot(anti, tile))

    for t in range(half // tr):
        lo = slice(t * tr, (t + 1) * tr)
        hi = slice(n - (t + 1) * tr, n - t * tr)
        for c, (dst, sign) in enumerate(((ue_ref, 1.0), (vo_ref, -1.0))):
            head = uv_ref[0, c, n - t * tr:n - t * tr + 1, :] if t else jnp.zeros((1, POOL_WIDTH), BF16)
            partner = reverse_after(uv_ref[0, c, hi, :], head)
            dst[lo, :] = (uv_ref[0, c, lo, :].astype(F32) + sign * partner).astype(BF16)

    for i in range(half // tk):
        rows = slice(i * tk, (i + 1) * tk)
        k = i * tk + lax.broadcasted_iota(jnp.int32, (tk, 1), 0)
        p = _dot(tab_ref[rows, :half], ue_ref[...]) + (1 - 2 * (k % 2)).astype(F32) * u_mid
        q = _dot(tab_ref[rows, half:], vo_ref[...])
        o_ref[0, rows, :] = (p - q).astype(BF16)
        dp_ref[rows, :] = (p + q).astype(BF16)

    j = lax.broadcasted_iota(jnp.int32, (SUBLANES, half), 1)
    alt = ((1 - 2 * (j % 2)).astype(F32) * inv).astype(BF16)
    mid = (_dot(alt, ue_ref[...])[0:1] + u_mid).astype(BF16)

    for s in range(half // tr):
        head = dp_ref[half - s * tr:half - s * tr + 1, :] if s else mid
        tile = dp_ref[half - (s + 1) * tr:half - s * tr, :]
        o_ref[0, half + s * tr:half + (s + 1) * tr, :] = reverse_after(tile, head).astype(BF16)


def _pos_dft(tab, uv):
    b, _, n, _ = uv.shape
    half = n // 2
    return pl.pallas_call(
        functools.partial(_pos_dft_kernel, n=n),
        out_shape=jax.ShapeDtypeStruct((b, n, POOL_WIDTH), BF16),
        grid=(b,),
        in_specs=[_resident((half, n)),
                  pl.BlockSpec((1, 2, n, POOL_WIDTH), lambda bi: (bi, 0, 0, 0), pipeline_mode=pl.Buffered(1))],
        out_specs=pl.BlockSpec((1, n, POOL_WIDTH), lambda bi: (bi, 0, 0)),
        scratch_shapes=[pltpu.VMEM((half, POOL_WIDTH), BF16)] * 3,
        compiler_params=_params("arbitrary"),
        name="pos_dft",
    )(tab, uv)


def _out_proj_kernel(x_ref, mod_ref, pu_ref, prev_ref, next_ref, pg_ref, fg_ref, mg_ref, r_ref, at_ref,
                     band_ref, pw_ref, ps_ref, fw_ref, wo_ref, o_ref, ext_ref, *, tm, n):
    i = pl.program_id(1)
    ext_ref[0:POOL_HALO] = jnp.where(i > 0, prev_ref[0], 0.0).astype(BF16)
    ext_ref[POOL_HALO:POOL_HALO + tm] = pu_ref[0].astype(BF16)
    ext_ref[POOL_HALO + tm:2 * POOL_HALO + tm] = jnp.where(i < pl.num_programs(1) - 1, next_ref[0], 0.0).astype(BF16)
    ext_ref[2 * POOL_HALO + tm:] = jnp.zeros((BAND_K - ROW_SUB - 2 * POOL_HALO, POOL_WIDTH), BF16)
    gate = mod_ref[0][:, 2 * D_MODEL:]
    pair = 2 * GROUP_CH
    for r0 in range(0, tm, ROW_SUB):
        rows = slice(r0, r0 + ROW_SUB)
        t = i * tm + r0 + lax.broadcasted_iota(jnp.int32, (ROW_SUB, 1), 0)
        pooled = []
        for g, w in enumerate(POOL_WINDOWS):
            cols = slice(g * GROUP_CH, (g + 1) * GROUP_CH)
            lo = w // 2
            hi = w - lo - 1
            s = _dot(band_ref[g], ext_ref[r0:r0 + BAND_K, cols])
            cnt = (jnp.minimum(t + hi, n - 1) - jnp.maximum(t - lo, 0) + 1).astype(F32)
            pooled.append((s / cnt - pu_ref[0, rows, cols]).astype(BF16))
        parts = []
        for j in range(N_GROUPS // 2):
            cols = slice(j * pair, (j + 1) * pair)
            y = _dot(jnp.concatenate(pooled[2 * j:2 * j + 2], axis=-1), pw_ref[j]) * ps_ref[:, cols]
            parts.append((_silu(pg_ref[0, rows, cols]) * y).astype(BF16))
        for j in range(N_GROUPS // 2):
            cols = slice(j * pair, (j + 1) * pair)
            f = _dot(r_ref[0, rows, cols], fw_ref[j])
            parts.append((_silu(fg_ref[0, rows, cols]) * f).astype(BF16))
        parts.append((_silu(mg_ref[0, rows, :]) * at_ref[0, rows, :].astype(F32)).astype(BF16))
        mixed = jnp.concatenate(parts, axis=-1)
        chunk = 512
        for c0 in range(0, D_MODEL, chunk):
            y = _dot(mixed, wo_ref[:, c0:c0 + chunk])
            o_ref[0, rows, c0:c0 + chunk] = x_ref[0, rows, c0:c0 + chunk] + gate[:, c0:c0 + chunk] * y


def _out_proj(xin, mod, p, r, attn, pool_w, pool_scale, fnet_w, w_out, *, layer, per_batch_mod, tm):
    b, n, _ = xin.shape
    mod_idx = (lambda bi, i: (bi, 0, 0)) if per_batch_mod else (lambda bi, i: (0, 0, 0))
    hb = tm // POOL_HALO
    last = n // POOL_HALO - 1
    pcol = lambda width, off: pl.BlockSpec((1, tm, width), lambda bi, i: (bi, i, off // width))
    pair = 2 * GROUP_CH
    return pl.pallas_call(
        functools.partial(_out_proj_kernel, tm=tm, n=n),
        out_shape=jax.ShapeDtypeStruct((b, n, D_MODEL), F32),
        grid=(b, n // tm),
        in_specs=[
            pl.BlockSpec((1, tm, D_MODEL), lambda bi, i: (bi, i, 0)),
            pl.BlockSpec((1, 1, 3 * D_MODEL), mod_idx),
            pcol(POOL_WIDTH, COL_POOL),
            pl.BlockSpec((1, POOL_HALO, POOL_WIDTH), lambda bi, i: (bi, jnp.maximum(i * hb - 1, 0), 0)),
            pl.BlockSpec((1, POOL_HALO, POOL_WIDTH), lambda bi, i: (bi, jnp.minimum((i + 1) * hb, last), 0)),
            pcol(POOL_WIDTH, COL_POOL_GATE),
            pcol(POOL_WIDTH, COL_FNET_GATE),
            pcol(MLA_WIDTH, COL_MLA_GATE),
            pl.BlockSpec((1, tm, POOL_WIDTH), lambda bi, i: (bi, i, 0)),
            pl.BlockSpec((1, tm, MLA_WIDTH), lambda bi, i: (bi, i, 0)),
            _resident((N_GROUPS, ROW_SUB, BAND_K)),
            _resident((N_GROUPS // 2, pair, pair), layer),
            _resident((1, POOL_WIDTH)),
            _resident((N_GROUPS // 2, pair, pair), layer),
            _resident((D_MODEL, D_MODEL), layer),
        ],
        out_specs=pl.BlockSpec((1, tm, D_MODEL), lambda bi, i: (bi, i, 0)),
        scratch_shapes=[pltpu.VMEM((tm + BAND_K - ROW_SUB, POOL_WIDTH), BF16)],
        compiler_params=_params("arbitrary", "arbitrary"),
        name="out_proj",
    )(xin, mod, p, p, p, p, p, p, r, attn, _pool_band_table(), pool_w, pool_scale, fnet_w, w_out)


def _rope_tables(n):
    t = jnp.arange(n, dtype=jnp.int32)
    inv_freq = ROPE_THETA ** (-jnp.arange(N_FREQ_PER_AXIS, dtype=F32) / N_FREQ_PER_AXIS)
    ang_r = (t // GRID_W).astype(F32)[:, None] * inv_freq
    ang_c = (t % GRID_W).astype(F32)[:, None] * inv_freq
    ang = jnp.concatenate([ang_r, ang_r, ang_c, ang_c], axis=-1)
    pad = jnp.zeros((n, LANES - QK_ROPE_DIM), F32)
    return (jnp.concatenate([jnp.cos(ang), pad], axis=-1),
            jnp.concatenate([jnp.sin(ang) * _ROT_SIGN, pad], axis=-1))


def _identity_rope_tables(n):
    pad = jnp.zeros((n, LANES - QK_ROPE_DIM), F32)
    return jnp.concatenate([jnp.ones((n, QK_ROPE_DIM), F32), pad], axis=-1), jnp.zeros((n, LANES), F32)


def _channel_dft_table():
    k = np.arange(GROUP_CH)
    ang = 2.0 * np.pi * ((k[:, None] * k[None, :]) % GROUP_CH) / GROUP_CH
    tab = np.concatenate([np.cos(ang), np.sin(ang)], axis=1) / math.sqrt(GROUP_CH)
    return jnp.asarray(tab, F32).astype(BF16)


def _position_dft_table(n):
    half = n // 2
    g = 1
    while g * g < n:
        g *= 2
    period = n // g
    j = jnp.arange(half, dtype=jnp.int32)
    a = jnp.arange(half // g, dtype=jnp.int32)
    bb = jnp.arange(g, dtype=jnp.int32)
    ang_a = ((a[:, None] * j[None, :]) % period).astype(F32) * (2.0 * math.pi / period)
    ang_b = ((bb[:, None] * j[None, :]) % n).astype(F32) * (2.0 * math.pi / n)
    ca, sa = jnp.cos(ang_a)[:, None, :], jnp.sin(ang_a)[:, None, :]
    cb, sb = jnp.cos(ang_b)[None, :, :], jnp.sin(ang_b)[None, :, :]
    scale = 1.0 / math.sqrt(n)
    cos = ((ca * cb - sa * sb) * scale).reshape(half, half)
    sin = ((sa * cb + ca * sb) * scale).reshape(half, half)
    return jnp.concatenate([cos, sin], axis=1).astype(BF16)


def _pool_band_table():
    t = np.arange(ROW_SUB)[:, None]
    d = np.arange(BAND_K)[None, :] - POOL_HALO - t
    bands = [((d >= -(w // 2)) & (d <= w - w // 2 - 1)).astype(np.float32) for w in POOL_WINDOWS]
    return jnp.asarray(np.stack(bands), BF16)


def _pair_block_diag(w):
    even, odd = w[:, 0::2], w[:, 1::2]
    zero = jnp.zeros_like(even)
    return jnp.concatenate([jnp.concatenate([even, zero], axis=-1),
                            jnp.concatenate([zero, odd], axis=-1)], axis=-2).astype(BF16)


def _pack_w_in(w):
    kr = w[..., 3072:3136]
    return jnp.concatenate([w[..., :3072], w[..., 3136:], kr, kr[..., _ROT_SRC]], axis=-1).astype(BF16)


def _pack_w_uq(w):
    w = w.reshape(DEPTH, LORA_RANK, MLA_HEADS, QK_HEAD_DIM)
    rope = w[..., QK_NOPE_DIM:]
    return jnp.concatenate([w[..., :QK_NOPE_DIM], rope, rope[..., _ROT_SRC]], axis=-1).reshape(
        DEPTH, LORA_RANK, MLA_HEADS * HEAD_PAD).astype(BF16)


def _pack_w_ukv(w):
    w = w.reshape(DEPTH, LORA_RANK, MLA_HEADS, QK_NOPE_DIM + V_HEAD_DIM)
    return jnp.concatenate([w[..., :QK_NOPE_DIM].reshape(DEPTH, LORA_RANK, -1),
                            w[..., QK_NOPE_DIM:].reshape(DEPTH, LORA_RANK, -1)], axis=-1).astype(BF16)


def _pack_head_w(qw, kw):
    pad = jnp.zeros((LANES - QK_ROPE_DIM,), F32)
    rows = []
    for w in (qw, kw):
        rope = w[QK_NOPE_DIM:]
        rows += [w[:QK_NOPE_DIM], jnp.concatenate([rope, pad]), jnp.concatenate([rope[_ROT_SRC], pad])]
    shift = QK_HEAD_DIM * SM_SCALE * LOG2_E * jnp.max(jnp.abs(qw)) * jnp.max(jnp.abs(kw))
    spare = (np.arange(LANES) == QK_ROPE_DIM).astype(np.float32)
    rows += [-shift * spare, jnp.asarray(spare)]
    return jnp.stack(rows), shift <= MAX_SAFE_SHIFT


def kernel(x, c, ctx, c_ctx, norm_w, w_ada, b_ada, w_in, pool_w, pool_scale, fnet_w, q_norm_w, w_uq,
           kv_norm_w, w_ukv, q_head_norm_w, k_head_norm_w, w_out):
    batch, seq, _ = x.shape
    n_ctx = ctx.shape[1]
    cc = jnp.concatenate([c, c_ctx[None, :], jnp.zeros((SUBLANES - batch - 1, D_MODEL), F32)], axis=0)
    mod_all = _ada_modulation(cc, w_ada, b_ada)

    cs_tab = _channel_dft_table()
    rope_lat = _rope_tables(seq)
    rope_ctx = _identity_rope_tables(n_ctx)
    dft_lat = _position_dft_table(seq)
    dft_ctx = _position_dft_table(n_ctx)
    tm_lat, tm_qkv, tm_ctx = 512, 512, n_ctx

    w_in_p, wuq, wukv = _pack_w_in(w_in), _pack_w_uq(w_uq), _pack_w_ukv(w_ukv)
    pw, fw, wo = _pair_block_diag(pool_w), _pair_block_diag(fnet_w), w_out.astype(BF16)

    for l in range(DEPTH):
        mod_lat = mod_all[l, :batch][:, None, :]
        mod_ctx = mod_all[l, batch:batch + 1][:, None, :]
        nw = norm_w[l][None, :]
        head_w, shift_is_safe = _pack_head_w(q_head_norm_w[l], k_head_norm_w[l])
        qnw, kvnw = q_norm_w[l][None, :], kv_norm_w[l][None, :]
        ps = pool_scale[l][None, :]
        ctx_out = l < DEPTH - 1

        p_c, uv_c = _in_proj(ctx, mod_ctx, nw, w_in_p, cs_tab, layer=l, per_batch_mod=False, tm=tm_ctx)
        q_c, k_c, v_c = _qkv(p_c, qnw, kvnw, wuq, wukv, head_w, *rope_ctx, layer=l, tm=tm_ctx)
        p_l, uv_l = _in_proj(x, mod_lat, nw, w_in_p, cs_tab, layer=l, per_batch_mod=True, tm=tm_lat)
        q_l, k_l, v_l = _qkv(p_l, qnw, kvnw, wuq, wukv, head_w, *rope_lat, layer=l, tm=tm_qkv)

        attn_l = _attention(shift_is_safe, q_l, k_c, v_c, k_l, v_l, tq=2048)
        r_l = _pos_dft(dft_lat, uv_l)
        x_new = _out_proj(x, mod_lat, p_l, r_l, attn_l, pw, ps, fw, wo, layer=l, per_batch_mod=True, tm=tm_lat)
        if ctx_out:
            attn_c = _attention(shift_is_safe, q_c, k_c, v_c, tq=n_ctx)
            r_c = _pos_dft(dft_ctx, uv_c)
            ctx = _out_proj(ctx, mod_ctx, p_c, r_c, attn_c, pw, ps, fw, wo, layer=l, per_batch_mod=False,
                            tm=tm_ctx)
        x = x_new
    return x
```

```python
import functools
import math

import numpy as np
import jax
import jax.numpy as jnp
from jax import lax
from jax.experimental import pallas as pl
from jax.experimental.pallas import tpu as pltpu

D_MODEL = 2048
DEPTH = 2
GRID_W = 64
POOL_WIDTH = 512
POOL_WINDOWS = (2, 4, 8, 16)
GROUP_CH = 128
N_GROUPS = 4
MLA_HEADS = 8
QK_NOPE_DIM = 128
QK_ROPE_DIM = 64
QK_HEAD_DIM = QK_NOPE_DIM + QK_ROPE_DIM
V_HEAD_DIM = 128
MLA_WIDTH = MLA_HEADS * V_HEAD_DIM
LORA_RANK = 512
N_FREQ_PER_AXIS = QK_ROPE_DIM // 4
ROPE_THETA = 10000.0
EPS = 1e-6
SM_SCALE = QK_HEAD_DIM ** -0.5
LOG2_E = math.log2(math.e)
MAX_SAFE_SHIFT = 60.0

COL_POOL = 0
COL_POOL_GATE = 512
COL_FNET = 1024
COL_FNET_GATE = 1536
COL_CQ = 2048
COL_CKV = 2560
COL_MLA_GATE = 3072
COL_KROPE = 4096
D_IN_PACKED = 4224
HEAD_PAD = 256

LANES = 128
SUBLANES = 8
VMEM_LIMIT = 56 * 1024 * 1024
ROW_SUB = 256
VT_ROWS = V_HEAD_DIM + 2 * SUBLANES
QKV_SUB = 256
POOL_HALO = 2 * SUBLANES
BAND_K = ROW_SUB + LANES

F32 = jnp.float32
BF16 = jnp.bfloat16

_ROT_SRC = np.array([i + 16 if (i % 32) < 16 else i - 16 for i in range(QK_ROPE_DIM)])
_ROT_SIGN = np.array([-1.0 if (i % 32) < 16 else 1.0 for i in range(QK_ROPE_DIM)], np.float32)


def _silu(x):
    return x * jax.nn.sigmoid(x)


def _dot(a, b):
    return jnp.dot(a, b, preferred_element_type=F32)


def _params(*sem):
    return pltpu.CompilerParams(dimension_semantics=sem, vmem_limit_bytes=VMEM_LIMIT)


def _resident(shape, layer=None):
    nd = len(shape)
    if layer is None:
        return pl.BlockSpec(shape, lambda *_: (0,) * nd, pipeline_mode=pl.Buffered(1))
    return pl.BlockSpec((None,) + tuple(shape), lambda *_: (layer,) + (0,) * nd, pipeline_mode=pl.Buffered(1))


def _ada_kernel(c_ref, w_ref, b_ref, o_ref):
    s = _silu(c_ref[...]).astype(BF16)
    o_ref[0] = _dot(s, w_ref[0].astype(BF16)) + b_ref[0]


def _ada_modulation(cc, w_ada, b_ada):
    tn = 512
    n3 = 3 * D_MODEL
    return pl.pallas_call(
        _ada_kernel,
        out_shape=jax.ShapeDtypeStruct((DEPTH, SUBLANES, n3), F32),
        grid=(DEPTH, n3 // tn),
        in_specs=[
            pl.BlockSpec((SUBLANES, D_MODEL), lambda l, j: (0, 0)),
            pl.BlockSpec((1, D_MODEL, tn), lambda l, j: (l, 0, j)),
            pl.BlockSpec((1, 1, tn), lambda l, j: (l, 0, j)),
        ],
        out_specs=pl.BlockSpec((1, SUBLANES, tn), lambda l, j: (l, 0, j)),
        compiler_params=_params("arbitrary", "arbitrary"),
        name="ada_modulation",
    )(cc, w_ada, b_ada.reshape(DEPTH, 1, n3))


def _in_proj_kernel(x_ref, mod_ref, nw_ref, w_ref, cs_ref, p_ref, uv_ref):
    mod = mod_ref[0]
    shift = mod[:, :D_MODEL]
    scale1 = (1.0 + mod[:, D_MODEL:2 * D_MODEL]) * nw_ref[...]
    tm = x_ref.shape[1]
    for r0 in range(0, tm, ROW_SUB):
        rows = slice(r0, min(r0 + ROW_SUB, tm))
        x = x_ref[0, rows, :]
        r = lax.rsqrt(jnp.mean(x * x, axis=-1, keepdims=True) + EPS)
        h = (x * r * scale1 + shift).astype(BF16)
        chunk = 512
        for c0 in range(0, D_IN_PACKED, chunk):
            c1 = min(c0 + chunk, D_IN_PACKED)
            pc = _dot(h, w_ref[:, c0:c1])
            p_ref[0, rows, c0:c1] = pc
            if c0 == COL_FNET:
                g = pc.astype(BF16)
                for hd in range(N_GROUPS):
                    cols = slice(hd * GROUP_CH, (hd + 1) * GROUP_CH)
                    uv = _dot(g[:, cols], cs_ref[...])
                    uv_ref[0, 0, rows, cols] = uv[:, :GROUP_CH].astype(BF16)
                    uv_ref[0, 1, rows, cols] = uv[:, GROUP_CH:].astype(BF16)


def _in_proj(xin, mod, norm_w, w_packed, cs_tab, *, layer, per_batch_mod, tm):
    b, n, _ = xin.shape
    mod_idx = (lambda bi, i: (bi, 0, 0)) if per_batch_mod else (lambda bi, i: (0, 0, 0))
    return pl.pallas_call(
        _in_proj_kernel,
        out_shape=(jax.ShapeDtypeStruct((b, n, D_IN_PACKED), F32),
                   jax.ShapeDtypeStruct((b, 2, n, POOL_WIDTH), BF16)),
        grid=(b, n // tm),
        in_specs=[
            pl.BlockSpec((1, tm, D_MODEL), lambda bi, i: (bi, i, 0)),
            pl.BlockSpec((1, 1, 3 * D_MODEL), mod_idx),
            _resident((1, D_MODEL)),
            _resident((D_MODEL, D_IN_PACKED), layer),
            _resident((GROUP_CH, 2 * GROUP_CH)),
        ],
        out_specs=(pl.BlockSpec((1, tm, D_IN_PACKED), lambda bi, i: (bi, i, 0)),
                   pl.BlockSpec((1, 2, tm, POOL_WIDTH), lambda bi, i: (bi, 0, i, 0))),
        compiler_params=_params("arbitrary", "arbitrary"),
        name="in_proj",
    )(xin, mod, norm_w, w_packed, cs_tab)


def _rms(x, w):
    return x * lax.rsqrt(jnp.mean(x * x, axis=-1, keepdims=True) + EPS) * w


def _rope(slab, t_cos, t_sin):
    return slab * t_cos + pltpu.roll(slab, LANES // 2, 1) * t_sin


def _normed_q_head(qh, cos, sin, hw):
    q_nope_w, q_rope_w, q_rot_w, q_pad = hw[0:1], hw[1:2], hw[2:3], hw[6:7]
    qn, qs = qh[:, :LANES], qh[:, LANES:]
    ss = jnp.sum(qn * qn + 0.5 * (qs * qs), axis=-1, keepdims=True)
    r = lax.rsqrt(ss * (1.0 / QK_HEAD_DIM) + EPS) * (SM_SCALE * LOG2_E)
    rope = _rope(qs, cos * q_rope_w, sin * q_rot_w) * r + q_pad
    return jnp.concatenate([(qn * r * q_nope_w).astype(BF16), rope.astype(BF16)], axis=-1)


def _qkv_kernel(cq_ref, ckv_ref, kr_ref, qnw_ref, kvnw_ref, wuq_ref, wukv_ref, hw_ref,
                cos_ref, sin_ref, q_ref, k_ref, v_ref):
    hw = hw_ref[...]
    k_nope_w, k_rope_w, k_rot_w, k_pad = hw[3:4], hw[4:5], hw[5:6], hw[7:8]
    tm = cq_ref.shape[1]
    sub = min(QKV_SUB, tm)
    one_row = (lax.broadcasted_iota(jnp.int32, (VT_ROWS - V_HEAD_DIM, sub), 0) == 0).astype(BF16)
    inv_dim = 1.0 / QK_HEAD_DIM

    for r0 in range(0, tm, sub):
        rows = slice(r0, r0 + sub)
        cos = cos_ref[rows, :]
        sin = sin_ref[rows, :]
        cqn = _rms(cq_ref[0, rows, :], qnw_ref[...]).astype(BF16)
        for hd in range(MLA_HEADS):
            cols = slice(hd * HEAD_PAD, (hd + 1) * HEAD_PAD)
            q_ref[0, rows, cols] = _normed_q_head(_dot(cqn, wuq_ref[:, cols]), cos, sin, hw)

        ckvn = _rms(ckv_ref[0, rows, :], kvnw_ref[...]).astype(BF16)
        ks = kr_ref[0, rows, :]
        ss_rope = 0.5 * jnp.sum(ks * ks, axis=-1, keepdims=True)
        k_rope = _rope(ks, cos * k_rope_w, sin * k_rot_w)
        for hd in range(MLA_HEADS):
            kn = _dot(ckvn, wukv_ref[:, hd * LANES:(hd + 1) * LANES])
            ss = jnp.sum(kn * kn, axis=-1, keepdims=True) + ss_rope
            r = lax.rsqrt(ss * inv_dim + EPS)
            k_ref[0, rows, hd * HEAD_PAD:hd * HEAD_PAD + LANES] = (kn * r * k_nope_w).astype(BF16)
            k_ref[0, rows, hd * HEAD_PAD + LANES:(hd + 1) * HEAD_PAD] = (k_rope * r + k_pad).astype(BF16)
        v = _dot(ckvn, wukv_ref[:, MLA_HEADS * LANES:])
        for hd in range(MLA_HEADS):
            v_ref[0, hd, 0:V_HEAD_DIM, rows] = v[:, hd * LANES:(hd + 1) * LANES].T.astype(BF16)
            v_ref[0, hd, V_HEAD_DIM:, rows] = one_row


def _qkv(p, q_norm_w, kv_norm_w, wuq, wukv, head_w, cos_tab, sin_tab, *, layer, tm):
    b, n, _ = p.shape
    row = lambda bi, i: (i, 0)
    return pl.pallas_call(
        _qkv_kernel,
        out_shape=(jax.ShapeDtypeStruct((b, n, MLA_HEADS * HEAD_PAD), BF16),
                   jax.ShapeDtypeStruct((b, n, MLA_HEADS * HEAD_PAD), BF16),
                   jax.ShapeDtypeStruct((b, MLA_HEADS, VT_ROWS, n), BF16)),
        grid=(b, n // tm),
        in_specs=[
            pl.BlockSpec((1, tm, LORA_RANK), lambda bi, i: (bi, i, COL_CQ // LORA_RANK)),
            pl.BlockSpec((1, tm, LORA_RANK), lambda bi, i: (bi, i, COL_CKV // LORA_RANK)),
            pl.BlockSpec((1, tm, LANES), lambda bi, i: (bi, i, COL_KROPE // LANES)),
            _resident((1, LORA_RANK)),
            _resident((1, LORA_RANK)),
            _resident((LORA_RANK, MLA_HEADS * HEAD_PAD), layer),
            _resident((LORA_RANK, 2 * MLA_WIDTH), layer),
            _resident((SUBLANES, LANES)),
            pl.BlockSpec((tm, LANES), row),
            pl.BlockSpec((tm, LANES), row),
        ],
        out_specs=(pl.BlockSpec((1, tm, MLA_HEADS * HEAD_PAD), lambda bi, i: (bi, i, 0)),
                   pl.BlockSpec((1, tm, MLA_HEADS * HEAD_PAD), lambda bi, i: (bi, i, 0)),
                   pl.BlockSpec((1, MLA_HEADS, VT_ROWS, tm), lambda bi, i: (bi, 0, 0, i))),
        compiler_params=_params("arbitrary", "arbitrary"),
        name="qkv",
    )(p, p, p, q_norm_w, kv_norm_w, wuq, wukv, head_w, cos_tab, sin_tab)


def _scores_t(k, q):
    return lax.dot_general(k, q, (((1,), (1,)), ((), ())), preferred_element_type=F32)


def _attn_finish(acc, o_ref, rows=slice(None)):
    o_ref[0, rows, :] = (acc[:V_HEAD_DIM] / acc[V_HEAD_DIM:V_HEAD_DIM + 1]).T.astype(BF16)


def _attn_shifted_kernel(*refs, tk, n_lat):
    if n_lat:
        q_ref, kc_ref, vc_ref, kl_ref, vl_ref, o_ref = refs
    else:
        q_ref, kc_ref, vc_ref, o_ref = refs
    tq = q_ref.shape[1]
    chains = [slice(r0, r0 + tq // 2) for r0 in (0, tq // 2)] if tq >= 2 * ROW_SUB else [slice(0, tq)]
    qs = [q_ref[0, rows, :] for rows in chains]

    def tile(q, k, vt):
        return _dot(vt, jnp.exp2(_scores_t(k, q)).astype(BF16))

    accs = [tile(q, kc_ref[0], vc_ref[0, 0]) for q in qs]
    for j in range(n_lat // tk):
        k, vt = kl_ref[0, j * tk:(j + 1) * tk, :], vl_ref[0, 0, :, j * tk:(j + 1) * tk]
        accs = [acc + tile(q, k, vt) for q, acc in zip(qs, accs)]
    for rows, acc in zip(chains, accs):
        _attn_finish(acc, o_ref, rows)


def _attn_online_kernel(*refs, tk, n_lat):
    if n_lat:
        q_ref, kc_ref, vc_ref, kl_ref, vl_ref, o_ref = refs
    else:
        q_ref, kc_ref, vc_ref, o_ref = refs
    q = q_ref[0]
    tq = q.shape[0]

    def step(k, vt, m, acc):
        s = _scores_t(k, q)
        m_new = jnp.maximum(m, jnp.max(s, axis=0, keepdims=True))
        p = jnp.exp2(s - m_new)
        return m_new, jnp.exp2(m - m_new) * acc + _dot(vt, p.astype(BF16))

    m = jnp.full((1, tq), -jnp.inf, F32)
    acc = jnp.zeros((VT_ROWS, tq), F32)
    m, acc = step(kc_ref[0], vc_ref[0, 0], m, acc)
    if n_lat:
        def body(j, carry):
            j0 = pl.multiple_of(j * tk, tk)
            return step(kl_ref[0, pl.ds(j0, tk), :], vl_ref[0, 0, :, pl.ds(j0, tk)], *carry)
        m, acc = lax.fori_loop(0, n_lat // tk, body, (m, acc))
    _attn_finish(acc, o_ref)


def _attention(shift_is_safe, q, k_ctx, v_ctx, k_lat=None, v_lat=None, *, tq, tk=512):
    b, nq, _ = q.shape
    n_ctx = k_ctx.shape[1]
    n_lat = 0 if k_lat is None else k_lat.shape[1]
    head = lambda bi, hd, i: (bi, 0, hd)
    head_t = lambda bi, hd, i: (bi, hd, 0, 0)
    in_specs = [
        pl.BlockSpec((1, tq, HEAD_PAD), lambda bi, hd, i: (bi, i, hd)),
        pl.BlockSpec((1, n_ctx, HEAD_PAD), head),
        pl.BlockSpec((1, 1, VT_ROWS, n_ctx), head_t),
    ]
    args = [q, k_ctx, v_ctx]
    if n_lat:
        in_specs += [pl.BlockSpec((1, n_lat, HEAD_PAD), head), pl.BlockSpec((1, 1, VT_ROWS, n_lat), head_t)]
        args += [k_lat, v_lat]

    def run(body, name):
        return pl.pallas_call(
            functools.partial(body, tk=tk, n_lat=n_lat),
            out_shape=jax.ShapeDtypeStruct((b, nq, MLA_WIDTH), BF16),
            grid=(b, MLA_HEADS, nq // tq),
            in_specs=in_specs,
            out_specs=pl.BlockSpec((1, tq, V_HEAD_DIM), lambda bi, hd, i: (bi, i, hd)),
            compiler_params=_params("arbitrary", "arbitrary", "arbitrary"),
            name=name,
        )

    return lax.cond(shift_is_safe,
                    lambda *a: run(_attn_shifted_kernel, "attention_shifted")(*a),
                    lambda *a: run(_attn_online_kernel, "attention_online")(*a),
                    *args)


def _pos_dft_kernel(tab_ref, uv_ref, o_ref, ue_ref, vo_ref, dp_ref, *, n):
    half = n // 2
    tr = min(ROW_SUB, half)
    tk = min(2 * ROW_SUB, half)
    inv = 1.0 / math.sqrt(n)
    u_mid = uv_ref[0, 0, half:half + 1, :].astype(F32) * inv
    anti = (lax.broadcasted_iota(jnp.int32, (tr, tr), 0) + lax.broadcasted_iota(jnp.int32, (tr, tr), 1)
            == tr).astype(BF16)
    first = lax.broadcasted_iota(jnp.int32, (tr, 1), 0) == 0

    def reverse_after(tile, head_row):
        return jnp.where(first, head_row.astype(F32), _dot(anti, tile))

    for t in range(half // tr):
        lo = slice(t * tr, (t + 1) * tr)
        hi = slice(n - (t + 1) * tr, n - t * tr)
        for c, (dst, sign) in enumerate(((ue_ref, 1.0), (vo_ref, -1.0))):
            head = uv_ref[0, c, n - t * tr:n - t * tr + 1, :] if t else jnp.zeros((1, POOL_WIDTH), BF16)
            partner = reverse_after(uv_ref[0, c, hi, :], head)
            dst[lo, :] = (uv_ref[0, c, lo, :].astype(F32) + sign * partner).astype(BF16)

    for i in range(half // tk):
        rows = slice(i * tk, (i + 1) * tk)
        k = i * tk + lax.broadcasted_iota(jnp.int32, (tk, 1), 0)
        p = _dot(tab_ref[rows, :half], ue_ref[...]) + (1 - 2 * (k % 2)).astype(F32) * u_mid
        q = _dot(tab_ref[rows, half:], vo_ref[...])
        o_ref[0, rows, :] = (p - q).astype(BF16)
        dp_ref[rows, :] = (p + q).astype(BF16)

    j = lax.broadcasted_iota(jnp.int32, (SUBLANES, half), 1)
    alt = ((1 - 2 * (j % 2)).astype(F32) * inv).astype(BF16)
    mid = (_dot(alt, ue_ref[...])[0:1] + u_mid).astype(BF16)

    for s in range(half // tr):
        head = dp_ref[half - s * tr:half - s * tr + 1, :] if s else mid
        tile = dp_ref[half - (s + 1) * tr:half - s * tr, :]
        o_ref[0, half + s * tr:half + (s + 1) * tr, :] = reverse_after(tile, head).astype(BF16)


def _pos_dft(tab, uv):
    b, _, n, _ = uv.shape
    half = n // 2
    return pl.pallas_call(
        functools.partial(_pos_dft_kernel, n=n),
        out_shape=jax.ShapeDtypeStruct((b, n, POOL_WIDTH), BF16),
        grid=(b,),
        in_specs=[_resident((half, n)),
                  pl.BlockSpec((1, 2, n, POOL_WIDTH), lambda bi: (bi, 0, 0, 0), pipeline_mode=pl.Buffered(1))],
        out_specs=pl.BlockSpec((1, n, POOL_WIDTH), lambda bi: (bi, 0, 0)),
        scratch_shapes=[pltpu.VMEM((half, POOL_WIDTH), BF16)] * 3,
        compiler_params=_params("arbitrary"),
        name="pos_dft",
    )(tab, uv)


def _out_proj_kernel(x_ref, mod_ref, pu_ref, prev_ref, next_ref, pg_ref, fg_ref, mg_ref, r_ref, at_ref,
                     band_ref, pw_ref, ps_ref, fw_ref, wo_ref, o_ref, ext_ref, *, tm, n):
    i = pl.program_id(1)
    ext_ref[0:POOL_HALO] = jnp.where(i > 0, prev_ref[0], 0.0).astype(BF16)
    ext_ref[POOL_HALO:POOL_HALO + tm] = pu_ref[0].astype(BF16)
    ext_ref[POOL_HALO + tm:2 * POOL_HALO + tm] = jnp.where(i < pl.num_programs(1) - 1, next_ref[0], 0.0).astype(BF16)
    ext_ref[2 * POOL_HALO + tm:] = jnp.zeros((BAND_K - ROW_SUB - 2 * POOL_HALO, POOL_WIDTH), BF16)
    gate = mod_ref[0][:, 2 * D_MODEL:]
    pair = 2 * GROUP_CH
    for r0 in range(0, tm, ROW_SUB):
        rows = slice(r0, r0 + ROW_SUB)
        t = i * tm + r0 + lax.broadcasted_iota(jnp.int32, (ROW_SUB, 1), 0)
        pooled = []
        for g, w in enumerate(POOL_WINDOWS):
            cols = slice(g * GROUP_CH, (g + 1) * GROUP_CH)
            lo = w // 2
            hi = w - lo - 1
            s = _dot(band_ref[g], ext_ref[r0:r0 + BAND_K, cols])
            cnt = (jnp.minimum(t + hi, n - 1) - jnp.maximum(t - lo, 0) + 1).astype(F32)
            pooled.append((s / cnt - pu_ref[0, rows, cols]).astype(BF16))
        parts = []
        for j in range(N_GROUPS // 2):
            cols = slice(j * pair, (j + 1) * pair)
            y = _dot(jnp.concatenate(pooled[2 * j:2 * j + 2], axis=-1), pw_ref[j]) * ps_ref[:, cols]
            parts.append((_silu(pg_ref[0, rows, cols]) * y).astype(BF16))
        for j in range(N_GROUPS // 2):
            cols = slice(j * pair, (j + 1) * pair)
            f = _dot(r_ref[0, rows, cols], fw_ref[j])
            parts.append((_silu(fg_ref[0, rows, cols]) * f).astype(BF16))
        parts.append((_silu(mg_ref[0, rows, :]) * at_ref[0, rows, :].astype(F32)).astype(BF16))
        mixed = jnp.concatenate(parts, axis=-1)
        chunk = 512
        for c0 in range(0, D_MODEL, chunk):
            y = _dot(mixed, wo_ref[:, c0:c0 + chunk])
            o_ref[0, rows, c0:c0 + chunk] = x_ref[0, rows, c0:c0 + chunk] + gate[:, c0:c0 + chunk] * y


def _out_proj(xin, mod, p, r, attn, pool_w, pool_scale, fnet_w, w_out, *, layer, per_batch_mod, tm):
    b, n, _ = xin.shape
    mod_idx = (lambda bi, i: (bi, 0, 0)) if per_batch_mod else (lambda bi, i: (0, 0, 0))
    hb = tm // POOL_HALO
    last = n // POOL_HALO - 1
    pcol = lambda width, off: pl.BlockSpec((1, tm, width), lambda bi, i: (bi, i, off // width))
    pair = 2 * GROUP_CH
    return pl.pallas_call(
        functools.partial(_out_proj_kernel, tm=tm, n=n),
        out_shape=jax.ShapeDtypeStruct((b, n, D_MODEL), F32),
        grid=(b, n // tm),
        in_specs=[
            pl.BlockSpec((1, tm, D_MODEL), lambda bi, i: (bi, i, 0)),
            pl.BlockSpec((1, 1, 3 * D_MODEL), mod_idx),
            pcol(POOL_WIDTH, COL_POOL),
            pl.BlockSpec((1, POOL_HALO, POOL_WIDTH), lambda bi, i: (bi, jnp.maximum(i * hb - 1, 0), 0)),
            pl.BlockSpec((1, POOL_HALO, POOL_WIDTH), lambda bi, i: (bi, jnp.minimum((i + 1) * hb, last), 0)),
            pcol(POOL_WIDTH, COL_POOL_GATE),
            pcol(POOL_WIDTH, COL_FNET_GATE),
            pcol(MLA_WIDTH, COL_MLA_GATE),
            pl.BlockSpec((1, tm, POOL_WIDTH), lambda bi, i: (bi, i, 0)),
            pl.BlockSpec((1, tm, MLA_WIDTH), lambda bi, i: (bi, i, 0)),
            _resident((N_GROUPS, ROW_SUB, BAND_K)),
            _resident((N_GROUPS // 2, pair, pair), layer),
            _resident((1, POOL_WIDTH)),
            _resident((N_GROUPS // 2, pair, pair), layer),
            _resident((D_MODEL, D_MODEL), layer),
        ],
        out_specs=pl.BlockSpec((1, tm, D_MODEL), lambda bi, i: (bi, i, 0)),
        scratch_shapes=[pltpu.VMEM((tm + BAND_K - ROW_SUB, POOL_WIDTH), BF16)],
        compiler_params=_params("arbitrary", "arbitrary"),
        name="out_proj",
    )(xin, mod, p, p, p, p, p, p, r, attn, _pool_band_table(), pool_w, pool_scale, fnet_w, w_out)


def _rope_tables(n):
    t = jnp.arange(n, dtype=jnp.int32)
    inv_freq = ROPE_THETA ** (-jnp.arange(N_FREQ_PER_AXIS, dtype=F32) / N_FREQ_PER_AXIS)
    ang_r = (t // GRID_W).astype(F32)[:, None] * inv_freq
    ang_c = (t % GRID_W).astype(F32)[:, None] * inv_freq
    ang = jnp.concatenate([ang_r, ang_r, ang_c, ang_c], axis=-1)
    pad = jnp.zeros((n, LANES - QK_ROPE_DIM), F32)
    return (jnp.concatenate([jnp.cos(ang), pad], axis=-1),
            jnp.concatenate([jnp.sin(ang) * _ROT_SIGN, pad], axis=-1))


def _identity_rope_tables(n):
    pad = jnp.zeros((n, LANES - QK_ROPE_DIM), F32)
    return jnp.concatenate([jnp.ones((n, QK_ROPE_DIM), F32), pad], axis=-1), jnp.zeros((n, LANES), F32)


def _channel_dft_table():
    k = np.arange(GROUP_CH)
    ang = 2.0 * np.pi * ((k[:, None] * k[None, :]) % GROUP_CH) / GROUP_CH
    tab = np.concatenate([np.cos(ang), np.sin(ang)], axis=1) / math.sqrt(GROUP_CH)
    return jnp.asarray(tab, F32).astype(BF16)


def _position_dft_table(n):
    half = n // 2
    g = 1
    while g * g < n:
        g *= 2
    period = n // g
    j = jnp.arange(half, dtype=jnp.int32)
    a = jnp.arange(half // g, dtype=jnp.int32)
    bb = jnp.arange(g, dtype=jnp.int32)
    ang_a = ((a[:, None] * j[None, :]) % period).astype(F32) * (2.0 * math.pi / period)
    ang_b = ((bb[:, None] * j[None, :]) % n).astype(F32) * (2.0 * math.pi / n)
    ca, sa = jnp.cos(ang_a)[:, None, :], jnp.sin(ang_a)[:, None, :]
    cb, sb = jnp.cos(ang_b)[None, :, :], jnp.sin(ang_b)[None, :, :]
    scale = 1.0 / math.sqrt(n)
    cos = ((ca * cb - sa * sb) * scale).reshape(half, half)
    sin = ((sa * cb + ca * sb) * scale).reshape(half, half)
    return jnp.concatenate([cos, sin], axis=1).astype(BF16)


def _pool_band_table():
    t = np.arange(ROW_SUB)[:, None]
    d = np.arange(BAND_K)[None, :] - POOL_HALO - t
    bands = [((d >= -(w // 2)) & (d <= w - w // 2 - 1)).astype(np.float32) for w in POOL_WINDOWS]
    return jnp.asarray(np.stack(bands), BF16)


def _pair_block_diag(w):
    even, odd = w[:, 0::2], w[:, 1::2]
    zero = jnp.zeros_like(even)
    return jnp.concatenate([jnp.concatenate([even, zero], axis=-1),
                            jnp.concatenate([zero, odd], axis=-1)], axis=-2).astype(BF16)


def _pack_w_in(w):
    kr = w[..., 3072:3136]
    return jnp.concatenate([w[..., :3072], w[..., 3136:], kr, kr[..., _ROT_SRC]], axis=-1).astype(BF16)


def _pack_w_uq(w):
    w = w.reshape(DEPTH, LORA_RANK, MLA_HEADS, QK_HEAD_DIM)
    rope = w[..., QK_NOPE_DIM:]
    return jnp.concatenate([w[..., :QK_NOPE_DIM], rope, rope[..., _ROT_SRC]], axis=-1).reshape(
        DEPTH, LORA_RANK, MLA_HEADS * HEAD_PAD).astype(BF16)


def _pack_w_ukv(w):
    w = w.reshape(DEPTH, LORA_RANK, MLA_HEADS, QK_NOPE_DIM + V_HEAD_DIM)
    return jnp.concatenate([w[..., :QK_NOPE_DIM].reshape(DEPTH, LORA_RANK, -1),
                            w[..., QK_NOPE_DIM:].reshape(DEPTH, LORA_RANK, -1)], axis=-1).astype(BF16)


def _pack_head_w(qw, kw):
    pad = jnp.zeros((LANES - QK_ROPE_DIM,), F32)
    rows = []
    for w in (qw, kw):
        rope = w[QK_NOPE_DIM:]
        rows += [w[:QK_NOPE_DIM], jnp.concatenate([rope, pad]), jnp.concatenate([rope[_ROT_SRC], pad])]
    shift = QK_HEAD_DIM * SM_SCALE * LOG2_E * jnp.max(jnp.abs(qw)) * jnp.max(jnp.abs(kw))
    spare = (np.arange(LANES) == QK_ROPE_DIM).astype(np.float32)
    rows += [-shift * spare, jnp.asarray(spare)]
    return jnp.stack(rows), shift <= MAX_SAFE_SHIFT


def kernel(x, c, ctx, c_ctx, norm_w, w_ada, b_ada, w_in, pool_w, pool_scale, fnet_w, q_norm_w, w_uq,
           kv_norm_w, w_ukv, q_head_norm_w, k_head_norm_w, w_out):
    batch, seq, _ = x.shape
    n_ctx = ctx.shape[1]
    cc = jnp.concatenate([c, c_ctx[None, :], jnp.zeros((SUBLANES - batch - 1, D_MODEL), F32)], axis=0)
    mod_all = _ada_modulation(cc, w_ada, b_ada)

    cs_tab = _channel_dft_table()
    rope_lat = _rope_tables(seq)
    rope_ctx = _identity_rope_tables(n_ctx)
    dft_lat = _position_dft_table(seq)
    dft_ctx = _position_dft_table(n_ctx)
    tm_lat, tm_qkv, tm_ctx = 512, 512, n_ctx

    w_in_p, wuq, wukv = _pack_w_in(w_in), _pack_w_uq(w_uq), _pack_w_ukv(w_ukv)
    pw, fw, wo = _pair_block_diag(pool_w), _pair_block_diag(fnet_w), w_out.astype(BF16)

    for l in range(DEPTH):
        mod_lat = mod_all[l, :batch][:, None, :]
        mod_ctx = mod_all[l, batch:batch + 1][:, None, :]
        nw = norm_w[l][None, :]
        head_w, shift_is_safe = _pack_head_w(q_head_norm_w[l], k_head_norm_w[l])
        qnw, kvnw = q_norm_w[l][None, :], kv_norm_w[l][None, :]
        ps = pool_scale[l][None, :]
        ctx_out = l < DEPTH - 1

        p_c, uv_c = _in_proj(ctx, mod_ctx, nw, w_in_p, cs_tab, layer=l, per_batch_mod=False, tm=tm_ctx)
        q_c, k_c, v_c = _qkv(p_c, qnw, kvnw, wuq, wukv, head_w, *rope_ctx, layer=l, tm=tm_ctx)
        p_l, uv_l = _in_proj(x, mod_lat, nw, w_in_p, cs_tab, layer=l, per_batch_mod=True, tm=tm_lat)
        q_l, k_l, v_l = _qkv(p_l, qnw, kvnw, wuq, wukv, head_w, *rope_lat, layer=l, tm=tm_qkv)

        attn_l = _attention(shift_is_safe, q_l, k_c, v_c, k_l, v_l, tq=2048)
        r_l = _pos_dft(dft_lat, uv_l)
        x_new = _out_proj(x, mod_lat, p_l, r_l, attn_l, pw, ps, fw, wo, layer=l, per_batch_mod=True, tm=tm_lat)
        if ctx_out:
            attn_c = _attention(shift_is_safe, q_c, k_c, v_c, tq=n_ctx)
            r_c = _pos_dft(dft_ctx, uv_c)
            ctx = _out_proj(ctx, mod_ctx, p_c, r_c, attn_c, pw, ps, fw, wo, layer=l, per_batch_mod=False,
                            tm=tm_ctx)
        x = x_new
    return x
```

```python
import functools
import math

import numpy as np
import jax
import jax.numpy as jnp
from jax import lax
from jax.experimental import pallas as pl
from jax.experimental.pallas import tpu as pltpu

D_MODEL = 2048
DEPTH = 2
GRID_W = 64
POOL_WIDTH = 512
POOL_WINDOWS = (2, 4, 8, 16)
GROUP_CH = 128
N_GROUPS = 4
MLA_HEADS = 8
QK_NOPE_DIM = 128
QK_ROPE_DIM = 64
QK_HEAD_DIM = QK_NOPE_DIM + QK_ROPE_DIM
V_HEAD_DIM = 128
MLA_WIDTH = MLA_HEADS * V_HEAD_DIM
LORA_RANK = 512
N_FREQ_PER_AXIS = QK_ROPE_DIM // 4
ROPE_THETA = 10000.0
EPS = 1e-6
SM_SCALE = QK_HEAD_DIM ** -0.5
LOG2_E = math.log2(math.e)
MAX_SAFE_SHIFT = 60.0

COL_POOL = 0
COL_POOL_GATE = 512
COL_FNET = 1024
COL_FNET_GATE = 1536
COL_CQ = 2048
COL_CKV = 2560
COL_MLA_GATE = 3072
COL_KROPE = 4096
D_IN_PACKED = 4224
HEAD_PAD = 256

LANES = 128
SUBLANES = 8
VMEM_LIMIT = 56 * 1024 * 1024
ROW_SUB = 256
VT_ROWS = V_HEAD_DIM + 2 * SUBLANES
QKV_SUB = 256
POOL_HALO = 2 * SUBLANES
BAND_K = ROW_SUB + LANES

F32 = jnp.float32
BF16 = jnp.bfloat16

_ROT_SRC = np.array([i + 16 if (i % 32) < 16 else i - 16 for i in range(QK_ROPE_DIM)])
_ROT_SIGN = np.array([-1.0 if (i % 32) < 16 else 1.0 for i in range(QK_ROPE_DIM)], np.float32)


def _silu(x):
    return x * jax.nn.sigmoid(x)


def _dot(a, b):
    return jnp.dot(a, b, preferred_element_type=F32)


def _params(*sem):
    return pltpu.CompilerParams(dimension_semantics=sem, vmem_limit_bytes=VMEM_LIMIT)


def _resident(shape, layer=None):
    nd = len(shape)
    if layer is None:
        return pl.BlockSpec(shape, lambda *_: (0,) * nd, pipeline_mode=pl.Buffered(1))
    return pl.BlockSpec((None,) + tuple(shape), lambda *_: (layer,) + (0,) * nd, pipeline_mode=pl.Buffered(1))


def _ada_kernel(c_ref, w_ref, b_ref, o_ref):
    s = _silu(c_ref[...]).astype(BF16)
    o_ref[0] = _dot(s, w_ref[0].astype(BF16)) + b_ref[0]


def _ada_modulation(cc, w_ada, b_ada):
    tn = 512
    n3 = 3 * D_MODEL
    return pl.pallas_call(
        _ada_kernel,
        out_shape=jax.ShapeDtypeStruct((DEPTH, SUBLANES, n3), F32),
        grid=(DEPTH, n3 // tn),
        in_specs=[
            pl.BlockSpec((SUBLANES, D_MODEL), lambda l, j: (0, 0)),
            pl.BlockSpec((1, D_MODEL, tn), lambda l, j: (l, 0, j)),
            pl.BlockSpec((1, 1, tn), lambda l, j: (l, 0, j)),
        ],
        out_specs=pl.BlockSpec((1, SUBLANES, tn), lambda l, j: (l, 0, j)),
        compiler_params=_params("arbitrary", "arbitrary"),
        name="ada_modulation",
    )(cc, w_ada, b_ada.reshape(DEPTH, 1, n3))


def _in_proj_kernel(x_ref, mod_ref, nw_ref, w_ref, cs_ref, p_ref, uv_ref):
    mod = mod_ref[0]
    shift = mod[:, :D_MODEL]
    scale1 = (1.0 + mod[:, D_MODEL:2 * D_MODEL]) * nw_ref[...]
    tm = x_ref.shape[1]
    for r0 in range(0, tm, ROW_SUB):
        rows = slice(r0, min(r0 + ROW_SUB, tm))
        x = x_ref[0, rows, :]
        r = lax.rsqrt(jnp.mean(x * x, axis=-1, keepdims=True) + EPS)
        h = (x * r * scale1 + shift).astype(BF16)
        chunk = 512
        for c0 in range(0, D_IN_PACKED, chunk):
            c1 = min(c0 + chunk, D_IN_PACKED)
            pc = _dot(h, w_ref[:, c0:c1])
            p_ref[0, rows, c0:c1] = pc
            if c0 == COL_FNET:
                g = pc.astype(BF16)
                for hd in range(N_GROUPS):
                    cols = slice(hd * GROUP_CH, (hd + 1) * GROUP_CH)
                    uv = _dot(g[:, cols], cs_ref[...])
                    uv_ref[0, 0, rows, cols] = uv[:, :GROUP_CH].astype(BF16)
                    uv_ref[0, 1, rows, cols] = uv[:, GROUP_CH:].astype(BF16)


def _in_proj(xin, mod, norm_w, w_packed, cs_tab, *, layer, per_batch_mod, tm):
    b, n, _ = xin.shape
    mod_idx = (lambda bi, i: (bi, 0, 0)) if per_batch_mod else (lambda bi, i: (0, 0, 0))
    return pl.pallas_call(
        _in_proj_kernel,
        out_shape=(jax.ShapeDtypeStruct((b, n, D_IN_PACKED), F32),
                   jax.ShapeDtypeStruct((b, 2, n, POOL_WIDTH), BF16)),
        grid=(b, n // tm),
        in_specs=[
            pl.BlockSpec((1, tm, D_MODEL), lambda bi, i: (bi, i, 0)),
            pl.BlockSpec((1, 1, 3 * D_MODEL), mod_idx),
            _resident((1, D_MODEL)),
            _resident((D_MODEL, D_IN_PACKED), layer),
            _resident((GROUP_CH, 2 * GROUP_CH)),
        ],
        out_specs=(pl.BlockSpec((1, tm, D_IN_PACKED), lambda bi, i: (bi, i, 0)),
                   pl.BlockSpec((1, 2, tm, POOL_WIDTH), lambda bi, i: (bi, 0, i, 0))),
        compiler_params=_params("arbitrary", "arbitrary"),
        name="in_proj",
    )(xin, mod, norm_w, w_packed, cs_tab)


def _rms(x, w):
    return x * lax.rsqrt(jnp.mean(x * x, axis=-1, keepdims=True) + EPS) * w


def _rope(slab, t_cos, t_sin):
    return slab * t_cos + pltpu.roll(slab, LANES // 2, 1) * t_sin


def _normed_q_head(qh, cos, sin, hw):
    q_nope_w, q_rope_w, q_rot_w, q_pad = hw[0:1], hw[1:2], hw[2:3], hw[6:7]
    qn, qs = qh[:, :LANES], qh[:, LANES:]
    ss = jnp.sum(qn * qn + 0.5 * (qs * qs), axis=-1, keepdims=True)
    r = lax.rsqrt(ss * (1.0 / QK_HEAD_DIM) + EPS) * (SM_SCALE * LOG2_E)
    rope = _rope(qs, cos * q_rope_w, sin * q_rot_w) * r + q_pad
    return jnp.concatenate([(qn * r * q_nope_w).astype(BF16), rope.astype(BF16)], axis=-1)


def _qkv_kernel(cq_ref, ckv_ref, kr_ref, qnw_ref, kvnw_ref, wuq_ref, wukv_ref, hw_ref,
                cos_ref, sin_ref, q_ref, k_ref, v_ref):
    hw = hw_ref[...]
    k_nope_w, k_rope_w, k_rot_w, k_pad = hw[3:4], hw[4:5], hw[5:6], hw[7:8]
    tm = cq_ref.shape[1]
    sub = min(QKV_SUB, tm)
    one_row = (lax.broadcasted_iota(jnp.int32, (VT_ROWS - V_HEAD_DIM, sub), 0) == 0).astype(BF16)
    inv_dim = 1.0 / QK_HEAD_DIM

    for r0 in range(0, tm, sub):
        rows = slice(r0, r0 + sub)
        cos = cos_ref[rows, :]
        sin = sin_ref[rows, :]
        cqn = _rms(cq_ref[0, rows, :], qnw_ref[...]).astype(BF16)
        for hd in range(MLA_HEADS):
            cols = slice(hd * HEAD_PAD, (hd + 1) * HEAD_PAD)
            q_ref[0, rows, cols] = _normed_q_head(_dot(cqn, wuq_ref[:, cols]), cos, sin, hw)

        ckvn = _rms(ckv_ref[0, rows, :], kvnw_ref[...]).astype(BF16)
        ks = kr_ref[0, rows, :]
        ss_rope = 0.5 * jnp.sum(ks * ks, axis=-1, keepdims=True)
        k_rope = _rope(ks, cos * k_rope_w, sin * k_rot_w)
        for hd in range(MLA_HEADS):
            kn = _dot(ckvn, wukv_ref[:, hd * LANES:(hd + 1) * LANES])
            ss = jnp.sum(kn * kn, axis=-1, keepdims=True) + ss_rope
            r = lax.rsqrt(ss * inv_dim + EPS)
            k_ref[0, rows, hd * HEAD_PAD:hd * HEAD_PAD + LANES] = (kn * r * k_nope_w).astype(BF16)
            k_ref[0, rows, hd * HEAD_PAD + LANES:(hd + 1) * HEAD_PAD] = (k_rope * r + k_pad).astype(BF16)
        v = _dot(ckvn, wukv_ref[:, MLA_HEADS * LANES:])
        for hd in range(MLA_HEADS):
            v_ref[0, hd, 0:V_HEAD_DIM, rows] = v[:, hd * LANES:(hd + 1) * LANES].T.astype(BF16)
            v_ref[0, hd, V_HEAD_DIM:, rows] = one_row


def _qkv(p, q_norm_w, kv_norm_w, wuq, wukv, head_w, cos_tab, sin_tab, *, layer, tm):
    b, n, _ = p.shape
    row = lambda bi, i: (i, 0)
    return pl.pallas_call(
        _qkv_kernel,
        out_shape=(jax.ShapeDtypeStruct((b, n, MLA_HEADS * HEAD_PAD), BF16),
                   jax.ShapeDtypeStruct((b, n, MLA_HEADS * HEAD_PAD), BF16),
                   jax.ShapeDtypeStruct((b, MLA_HEADS, VT_ROWS, n), BF16)),
        grid=(b, n // tm),
        in_specs=[
            pl.BlockSpec((1, tm, LORA_RANK), lambda bi, i: (bi, i, COL_CQ // LORA_RANK)),
            pl.BlockSpec((1, tm, LORA_RANK), lambda bi, i: (bi, i, COL_CKV // LORA_RANK)),
            pl.BlockSpec((1, tm, LANES), lambda bi, i: (bi, i, COL_KROPE // LANES)),
            _resident((1, LORA_RANK)),
            _resident((1, LORA_RANK)),
            _resident((LORA_RANK, MLA_HEADS * HEAD_PAD), layer),
            _resident((LORA_RANK, 2 * MLA_WIDTH), layer),
            _resident((SUBLANES, LANES)),
            pl.BlockSpec((tm, LANES), row),
            pl.BlockSpec((tm, LANES), row),
        ],
        out_specs=(pl.BlockSpec((1, tm, MLA_HEADS * HEAD_PAD), lambda bi, i: (bi, i, 0)),
                   pl.BlockSpec((1, tm, MLA_HEADS * HEAD_PAD), lambda bi, i: (bi, i, 0)),
                   pl.BlockSpec((1, MLA_HEADS, VT_ROWS, tm), lambda bi, i: (bi, 0, 0, i))),
        compiler_params=_params("arbitrary", "arbitrary"),
        name="qkv",
    )(p, p, p, q_norm_w, kv_norm_w, wuq, wukv, head_w, cos_tab, sin_tab)


def _scores_t(k, q):
    return lax.dot_general(k, q, (((1,), (1,)), ((), ())), preferred_element_type=F32)


def _attn_finish(acc, o_ref, rows=slice(None)):
    o_ref[0, rows, :] = (acc[:V_HEAD_DIM] / acc[V_HEAD_DIM:V_HEAD_DIM + 1]).T.astype(BF16)


def _attn_shifted_kernel(*refs, tk, n_lat):
    if n_lat:
        q_ref, kc_ref, vc_ref, kl_ref, vl_ref, o_ref = refs
    else:
        q_ref, kc_ref, vc_ref, o_ref = refs
    tq = q_ref.shape[1]
    chains = [slice(r0, r0 + tq // 2) for r0 in (0, tq // 2)] if tq >= 2 * ROW_SUB else [slice(0, tq)]
    qs = [q_ref[0, rows, :] for rows in chains]

    def tile(q, k, vt):
        return _dot(vt, jnp.exp2(_scores_t(k, q)).astype(BF16))

    accs = [tile(q, kc_ref[0], vc_ref[0, 0]) for q in qs]
    for j in range(n_lat // tk):
        k, vt = kl_ref[0, j * tk:(j + 1) * tk, :], vl_ref[0, 0, :, j * tk:(j + 1) * tk]
        accs = [acc + tile(q, k, vt) for q, acc in zip(qs, accs)]
    for rows, acc in zip(chains, accs):
        _attn_finish(acc, o_ref, rows)


def _attn_online_kernel(*refs, tk, n_lat):
    if n_lat:
        q_ref, kc_ref, vc_ref, kl_ref, vl_ref, o_ref = refs
    else:
        q_ref, kc_ref, vc_ref, o_ref = refs
    q = q_ref[0]
    tq = q.shape[0]

    def step(k, vt, m, acc):
        s = _scores_t(k, q)
        m_new = jnp.maximum(m, jnp.max(s, axis=0, keepdims=True))
        p = jnp.exp2(s - m_new)
        return m_new, jnp.exp2(m - m_new) * acc + _dot(vt, p.astype(BF16))

    m = jnp.full((1, tq), -jnp.inf, F32)
    acc = jnp.zeros((VT_ROWS, tq), F32)
    m, acc = step(kc_ref[0], vc_ref[0, 0], m, acc)
    if n_lat:
        def body(j, carry):
            j0 = pl.multiple_of(j * tk, tk)
            return step(kl_ref[0, pl.ds(j0, tk), :], vl_ref[0, 0, :, pl.ds(j0, tk)], *carry)
        m, acc = lax.fori_loop(0, n_lat // tk, body, (m, acc))
    _attn_finish(acc, o_ref)


def _attention(shift_is_safe, q, k_ctx, v_ctx, k_lat=None, v_lat=None, *, tq, tk=512):
    b, nq, _ = q.shape
    n_ctx = k_ctx.shape[1]
    n_lat = 0 if k_lat is None else k_lat.shape[1]
    head = lambda bi, hd, i: (bi, 0, hd)
    head_t = lambda bi, hd, i: (bi, hd, 0, 0)
    in_specs = [
        pl.BlockSpec((1, tq, HEAD_PAD), lambda bi, hd, i: (bi, i, hd)),
        pl.BlockSpec((1, n_ctx, HEAD_PAD), head),
        pl.BlockSpec((1, 1, VT_ROWS, n_ctx), head_t),
    ]
    args = [q, k_ctx, v_ctx]
    if n_lat:
        in_specs += [pl.BlockSpec((1, n_lat, HEAD_PAD), head), pl.BlockSpec((1, 1, VT_ROWS, n_lat), head_t)]
        args += [k_lat, v_lat]

    def run(body, name):
        return pl.pallas_call(
            functools.partial(body, tk=tk, n_lat=n_lat),
            out_shape=jax.ShapeDtypeStruct((b, nq, MLA_WIDTH), BF16),
            grid=(b, MLA_HEADS, nq // tq),
            in_specs=in_specs,
            out_specs=pl.BlockSpec((1, tq, V_HEAD_DIM), lambda bi, hd, i: (bi, i, hd)),
            compiler_params=_params("arbitrary", "arbitrary", "arbitrary"),
            name=name,
        )

    return lax.cond(shift_is_safe,
                    lambda *a: run(_attn_shifted_kernel, "attention_shifted")(*a),
                    lambda *a: run(_attn_online_kernel, "attention_online")(*a),
                    *args)


def _pos_dft_kernel(tab_ref, uv_ref, o_ref, ue_ref, vo_ref, dp_ref, *, n):
    half = n // 2
    tr = min(ROW_SUB, half)
    tk = min(2 * ROW_SUB, half)
    inv = 1.0 / math.sqrt(n)
    u_mid = uv_ref[0, 0, half:half + 1, :].astype(F32) * inv
    anti = (lax.broadcasted_iota(jnp.int32, (tr, tr), 0) + lax.broadcasted_iota(jnp.int32, (tr, tr), 1)
            == tr).astype(BF16)
    first = lax.broadcasted_iota(jnp.int32, (tr, 1), 0) == 0

    def reverse_after(tile, head_row):
        return jnp.where(first, head_row.astype(F32), _dot(anti, tile))

    for t in range(half // tr):
        lo = slice(t * tr, (t + 1) * tr)
        hi = slice(n - (t + 1) * tr, n - t * tr)
        for c, (dst, sign) in enumerate(((ue_ref, 1.0), (vo_ref, -1.0))):
            head = uv_ref[0, c, n - t * tr:n - t * tr + 1, :] if t else jnp.zeros((1, POOL_WIDTH), BF16)
            partner = reverse_after(uv_ref[0, c, hi, :], head)
            dst[lo, :] = (uv_ref[0, c, lo, :].astype(F32) + sign * partner).astype(BF16)

    for i in range(half // tk):
        rows = slice(i * tk, (i + 1) * tk)
        k = i * tk + lax.broadcasted_iota(jnp.int32, (tk, 1), 0)
        p = _dot(tab_ref[rows, :half], ue_ref[...]) + (1 - 2 * (k % 2)).astype(F32) * u_mid
        q = _dot(tab_ref[rows, half:], vo_ref[...])
        o_ref[0, rows, :] = (p - q).astype(BF16)
        dp_ref[rows, :] = (p + q).astype(BF16)

    j = lax.broadcasted_iota(jnp.int32, (SUBLANES, half), 1)
    alt = ((1 - 2 * (j % 2)).astype(F32) * inv).astype(BF16)
    mid = (_dot(alt, ue_ref[...])[0:1] + u_mid).astype(BF16)

    for s in range(half // tr):
        head = dp_ref[half - s * tr:half - s * tr + 1, :] if s else mid
        tile = dp_ref[half - (s + 1) * tr:half - s * tr, :]
        o_ref[0, half + s * tr:half + (s + 1) * tr, :] = reverse_after(tile, head).astype(BF16)


def _pos_dft(tab, uv):
    b, _, n, _ = uv.shape
    half = n // 2
    return pl.pallas_call(
        functools.partial(_pos_dft_kernel, n=n),
        out_shape=jax.ShapeDtypeStruct((b, n, POOL_WIDTH), BF16),
        grid=(b,),
        in_specs=[_resident((half, n)),
                  pl.BlockSpec((1, 2, n, POOL_WIDTH), lambda bi: (bi, 0, 0, 0), pipeline_mode=pl.Buffered(1))],
        out_specs=pl.BlockSpec((1, n, POOL_WIDTH), lambda bi: (bi, 0, 0)),
        scratch_shapes=[pltpu.VMEM((half, POOL_WIDTH), BF16)] * 3,
        compiler_params=_params("arbitrary"),
        name="pos_dft",
    )(tab, uv)


def _out_proj_kernel(x_ref, mod_ref, pu_ref, prev_ref, next_ref, pg_ref, fg_ref, mg_ref, r_ref, at_ref,
                     band_ref, pw_ref, ps_ref, fw_ref, wo_ref, o_ref, ext_ref, *, tm, n):
    i = pl.program_id(1)
    ext_ref[0:POOL_HALO] = jnp.where(i > 0, prev_ref[0], 0.0).astype(BF16)
    ext_ref[POOL_HALO:POOL_HALO + tm] = pu_ref[0].astype(BF16)
    ext_ref[POOL_HALO + tm:2 * POOL_HALO + tm] = jnp.where(i < pl.num_programs(1) - 1, next_ref[0], 0.0).astype(BF16)
    ext_ref[2 * POOL_HALO + tm:] = jnp.zeros((BAND_K - ROW_SUB - 2 * POOL_HALO, POOL_WIDTH), BF16)
    gate = mod_ref[0][:, 2 * D_MODEL:]
    pair = 2 * GROUP_CH
    for r0 in range(0, tm, ROW_SUB):
        rows = slice(r0, r0 + ROW_SUB)
        t = i * tm + r0 + lax.broadcasted_iota(jnp.int32, (ROW_SUB, 1), 0)
        pooled = []
        for g, w in enumerate(POOL_WINDOWS):
            cols = slice(g * GROUP_CH, (g + 1) * GROUP_CH)
            lo = w // 2
            hi = w - lo - 1
            s = _dot(band_ref[g], ext_ref[r0:r0 + BAND_K, cols])
            cnt = (jnp.minimum(t + hi, n - 1) - jnp.maximum(t - lo, 0) + 1).astype(F32)
            pooled.append((s / cnt - pu_ref[0, rows, cols]).astype(BF16))
        parts = []
        for j in range(N_GROUPS // 2):
            cols = slice(j * pair, (j + 1) * pair)
            y = _dot(jnp.concatenate(pooled[2 * j:2 * j + 2], axis=-1), pw_ref[j]) * ps_ref[:, cols]
            parts.append((_silu(pg_ref[0, rows, cols]) * y).astype(BF16))
        for j in range(N_GROUPS // 2):
            cols = slice(j * pair, (j + 1) * pair)
            f = _dot(r_ref[0, rows, cols], fw_ref[j])
            parts.append((_silu(fg_ref[0, rows, cols]) * f).astype(BF16))
        parts.append((_silu(mg_ref[0, rows, :]) * at_ref[0, rows, :].astype(F32)).astype(BF16))
        mixed = jnp.concatenate(parts, axis=-1)
        chunk = 512
        for c0 in range(0, D_MODEL, chunk):
            y = _dot(mixed, wo_ref[:, c0:c0 + chunk])
            o_ref[0, rows, c0:c0 + chunk] = x_ref[0, rows, c0:c0 + chunk] + gate[:, c0:c0 + chunk] * y


def _out_proj(xin, mod, p, r, attn, pool_w, pool_scale, fnet_w, w_out, *, layer, per_batch_mod, tm):
    b, n, _ = xin.shape
    mod_idx = (lambda bi, i: (bi, 0, 0)) if per_batch_mod else (lambda bi, i: (0, 0, 0))
    hb = tm // POOL_HALO
    last = n // POOL_HALO - 1
    pcol = lambda width, off: pl.BlockSpec((1, tm, width), lambda bi, i: (bi, i, off // width))
    pair = 2 * GROUP_CH
    return pl.pallas_call(
        functools.partial(_out_proj_kernel, tm=tm, n=n),
        out_shape=jax.ShapeDtypeStruct((b, n, D_MODEL), F32),
        grid=(b, n // tm),
        in_specs=[
            pl.BlockSpec((1, tm, D_MODEL), lambda bi, i: (bi, i, 0)),
            pl.BlockSpec((1, 1, 3 * D_MODEL), mod_idx),
            pcol(POOL_WIDTH, COL_POOL),
            pl.BlockSpec((1, POOL_HALO, POOL_WIDTH), lambda bi, i: (bi, jnp.maximum(i * hb - 1, 0), 0)),
            pl.BlockSpec((1, POOL_HALO, POOL_WIDTH), lambda bi, i: (bi, jnp.minimum((i + 1) * hb, last), 0)),
            pcol(POOL_WIDTH, COL_POOL_GATE),
            pcol(POOL_WIDTH, COL_FNET_GATE),
            pcol(MLA_WIDTH, COL_MLA_GATE),
            pl.BlockSpec((1, tm, POOL_WIDTH), lambda bi, i: (bi, i, 0)),
            pl.BlockSpec((1, tm, MLA_WIDTH), lambda bi, i: (bi, i, 0)),
            _resident((N_GROUPS, ROW_SUB, BAND_K)),
            _resident((N_GROUPS // 2, pair, pair), layer),
            _resident((1, POOL_WIDTH)),
            _resident((N_GROUPS // 2, pair, pair), layer),
            _resident((D_MODEL, D_MODEL), layer),
        ],
        out_specs=pl.BlockSpec((1, tm, D_MODEL), lambda bi, i: (bi, i, 0)),
        scratch_shapes=[pltpu.VMEM((tm + BAND_K - ROW_SUB, POOL_WIDTH), BF16)],
        compiler_params=_params("arbitrary", "arbitrary"),
        name="out_proj",
    )(xin, mod, p, p, p, p, p, p, r, attn, _pool_band_table(), pool_w, pool_scale, fnet_w, w_out)


def _rope_tables(n):
    t = jnp.arange(n, dtype=jnp.int32)
    inv_freq = ROPE_THETA ** (-jnp.arange(N_FREQ_PER_AXIS, dtype=F32) / N_FREQ_PER_AXIS)
    ang_r = (t // GRID_W).astype(F32)[:, None] * inv_freq
    ang_c = (t % GRID_W).astype(F32)[:, None] * inv_freq
    ang = jnp.concatenate([ang_r, ang_r, ang_c, ang_c], axis=-1)
    pad = jnp.zeros((n, LANES - QK_ROPE_DIM), F32)
    return (jnp.concatenate([jnp.cos(ang), pad], axis=-1),
            jnp.concatenate([jnp.sin(ang) * _ROT_SIGN, pad], axis=-1))


def _identity_rope_tables(n):
    pad = jnp.zeros((n, LANES - QK_ROPE_DIM), F32)
    return jnp.concatenate([jnp.ones((n, QK_ROPE_DIM), F32), pad], axis=-1), jnp.zeros((n, LANES), F32)


def _channel_dft_table():
    k = np.arange(GROUP_CH)
    ang = 2.0 * np.pi * ((k[:, None] * k[None, :]) % GROUP_CH) / GROUP_CH
    tab = np.concatenate([np.cos(ang), np.sin(ang)], axis=1) / math.sqrt(GROUP_CH)
    return jnp.asarray(tab, F32).astype(BF16)


def _position_dft_table(n):
    half = n // 2
    g = 1
    while g * g < n:
        g *= 2
    period = n // g
    j = jnp.arange(half, dtype=jnp.int32)
    a = jnp.arange(half // g, dtype=jnp.int32)
    bb = jnp.arange(g, dtype=jnp.int32)
    ang_a = ((a[:, None] * j[None, :]) % period).astype(F32) * (2.0 * math.pi / period)
    ang_b = ((bb[:, None] * j[None, :]) % n).astype(F32) * (2.0 * math.pi / n)
    ca, sa = jnp.cos(ang_a)[:, None, :], jnp.sin(ang_a)[:, None, :]
    cb, sb = jnp.cos(ang_b)[None, :, :], jnp.sin(ang_b)[None, :, :]
    scale = 1.0 / math.sqrt(n)
    cos = ((ca * cb - sa * sb) * scale).reshape(half, half)
    sin = ((sa * cb + ca * sb) * scale).reshape(half, half)
    return jnp.concatenate([cos, sin], axis=1).astype(BF16)


def _pool_band_table():
    t = np.arange(ROW_SUB)[:, None]
    d = np.arange(BAND_K)[None, :] - POOL_HALO - t
    bands = [((d >= -(w // 2)) & (d <= w - w // 2 - 1)).astype(np.float32) for w in POOL_WINDOWS]
    return jnp.asarray(np.stack(bands), BF16)


def _pair_block_diag(w):
    even, odd = w[:, 0::2], w[:, 1::2]
    zero = jnp.zeros_like(even)
    return jnp.concatenate([jnp.concatenate([even, zero], axis=-1),
                            jnp.concatenate([zero, odd], axis=-1)], axis=-2).astype(BF16)


def _pack_w_in_kernel(w_ref, o_ref):
    w = w_ref[0]
    rows = w.shape[0]
    o_ref[0, :, :COL_MLA_GATE] = w[:, :COL_MLA_GATE].astype(BF16)
    o_ref[0, :, COL_MLA_GATE:COL_KROPE] = w[:, COL_MLA_GATE + QK_ROPE_DIM:].astype(BF16)
    x = w[:, COL_MLA_GATE:COL_MLA_GATE + LANES]
    lane = lax.broadcasted_iota(jnp.int32, (rows, LANES), 1)
    src = jnp.where(lane % 32 < 16, pltpu.roll(x, LANES - 16, 1), pltpu.roll(x, 16, 1))
    o_ref[0, :, COL_KROPE:] = jnp.where(lane < QK_ROPE_DIM, x, pltpu.roll(src, QK_ROPE_DIM, 1)).astype(BF16)


def _pack_w_in(w):
    depth, d, d_in = w.shape
    tr = 256
    return pl.pallas_call(
        _pack_w_in_kernel,
        out_shape=jax.ShapeDtypeStruct((depth, d, D_IN_PACKED), BF16),
        grid=(depth, d // tr),
        in_specs=[pl.BlockSpec((1, tr, d_in), lambda l, i: (l, i, 0))],
        out_specs=pl.BlockSpec((1, tr, D_IN_PACKED), lambda l, i: (l, i, 0)),
        compiler_params=_params("arbitrary", "arbitrary"),
        name="pack_w_in",
    )(w)


def _pack_w_uq(w):
    w = w.reshape(DEPTH, LORA_RANK, MLA_HEADS, QK_HEAD_DIM)
    rope = w[..., QK_NOPE_DIM:]
    return jnp.concatenate([w[..., :QK_NOPE_DIM], rope, rope[..., _ROT_SRC]], axis=-1).reshape(
        DEPTH, LORA_RANK, MLA_HEADS * HEAD_PAD).astype(BF16)


def _pack_w_ukv(w):
    w = w.reshape(DEPTH, LORA_RANK, MLA_HEADS, QK_NOPE_DIM + V_HEAD_DIM)
    return jnp.concatenate([w[..., :QK_NOPE_DIM].reshape(DEPTH, LORA_RANK, -1),
                            w[..., QK_NOPE_DIM:].reshape(DEPTH, LORA_RANK, -1)], axis=-1).astype(BF16)


def _pack_head_w(qw, kw):
    pad = jnp.zeros((LANES - QK_ROPE_DIM,), F32)
    rows = []
    for w in (qw, kw):
        rope = w[QK_NOPE_DIM:]
        rows += [w[:QK_NOPE_DIM], jnp.concatenate([rope, pad]), jnp.concatenate([rope[_ROT_SRC], pad])]
    shift = QK_HEAD_DIM * SM_SCALE * LOG2_E * jnp.max(jnp.abs(qw)) * jnp.max(jnp.abs(kw))
    spare = (np.arange(LANES) == QK_ROPE_DIM).astype(np.float32)
    rows += [-shift * spare, jnp.asarray(spare)]
    return jnp.stack(rows), shift <= MAX_SAFE_SHIFT


def kernel(x, c, ctx, c_ctx, norm_w, w_ada, b_ada, w_in, pool_w, pool_scale, fnet_w, q_norm_w, w_uq,
           kv_norm_w, w_ukv, q_head_norm_w, k_head_norm_w, w_out):
    batch, seq, _ = x.shape
    n_ctx = ctx.shape[1]
    cc = jnp.concatenate([c, c_ctx[None, :], jnp.zeros((SUBLANES - batch - 1, D_MODEL), F32)], axis=0)
    mod_all = _ada_modulation(cc, w_ada, b_ada)

    cs_tab = _channel_dft_table()
    rope_lat = _rope_tables(seq)
    rope_ctx = _identity_rope_tables(n_ctx)
    dft_lat = _position_dft_table(seq)
    dft_ctx = _position_dft_table(n_ctx)
    tm_lat, tm_qkv, tm_ctx = 512, 512, n_ctx

    w_in_p, wuq, wukv = _pack_w_in(w_in), _pack_w_uq(w_uq), _pack_w_ukv(w_ukv)
    pw, fw, wo = _pair_block_diag(pool_w), _pair_block_diag(fnet_w), w_out.astype(BF16)

    for l in range(DEPTH):
        mod_lat = mod_all[l, :batch][:, None, :]
        mod_ctx = mod_all[l, batch:batch + 1][:, None, :]
        nw = norm_w[l][None, :]
        head_w, shift_is_safe = _pack_head_w(q_head_norm_w[l], k_head_norm_w[l])
        qnw, kvnw = q_norm_w[l][None, :], kv_norm_w[l][None, :]
        ps = pool_scale[l][None, :]
        ctx_out = l < DEPTH - 1

        p_c, uv_c = _in_proj(ctx, mod_ctx, nw, w_in_p, cs_tab, layer=l, per_batch_mod=False, tm=tm_ctx)
        q_c, k_c, v_c = _qkv(p_c, qnw, kvnw, wuq, wukv, head_w, *rope_ctx, layer=l, tm=tm_ctx)
        p_l, uv_l = _in_proj(x, mod_lat, nw, w_in_p, cs_tab, layer=l, per_batch_mod=True, tm=tm_lat)
        q_l, k_l, v_l = _qkv(p_l, qnw, kvnw, wuq, wukv, head_w, *rope_lat, layer=l, tm=tm_qkv)

        attn_l = _attention(shift_is_safe, q_l, k_c, v_c, k_l, v_l, tq=2048)
        r_l = _pos_dft(dft_lat, uv_l)
        x_new = _out_proj(x, mod_lat, p_l, r_l, attn_l, pw, ps, fw, wo, layer=l, per_batch_mod=True, tm=tm_lat)
        if ctx_out:
            attn_c = _attention(shift_is_safe, q_c, k_c, v_c, tq=n_ctx)
            r_c = _pos_dft(dft_ctx, uv_c)
            ctx = _out_proj(ctx, mod_ctx, p_c, r_c, attn_c, pw, ps, fw, wo, layer=l, per_batch_mod=False,
                            tm=tm_ctx)
        x = x_new
    return x
```

```python
import functools
import math

import numpy as np
import jax
import jax.numpy as jnp
from jax import lax
from jax.experimental import pallas as pl
from jax.experimental.pallas import tpu as pltpu

D_MODEL = 2048
DEPTH = 2
GRID_W = 64
POOL_WIDTH = 512
POOL_WINDOWS = (2, 4, 8, 16)
GROUP_CH = 128
N_GROUPS = 4
MLA_HEADS = 8
QK_NOPE_DIM = 128
QK_ROPE_DIM = 64
QK_HEAD_DIM = QK_NOPE_DIM + QK_ROPE_DIM
V_HEAD_DIM = 128
MLA_WIDTH = MLA_HEADS * V_HEAD_DIM
LORA_RANK = 512
N_FREQ_PER_AXIS = QK_ROPE_DIM // 4
ROPE_THETA = 10000.0
EPS = 1e-6
SM_SCALE = QK_HEAD_DIM ** -0.5
LOG2_E = math.log2(math.e)
MAX_SAFE_SHIFT = 60.0

COL_POOL = 0
COL_POOL_GATE = 512
COL_FNET = 1024
COL_FNET_GATE = 1536
COL_CQ = 2048
COL_CKV = 2560
COL_MLA_GATE = 3072
COL_KROPE = 4096
D_IN_PACKED = 4224
HEAD_PAD = 256

LANES = 128
SUBLANES = 8
VMEM_LIMIT = 56 * 1024 * 1024
ROW_SUB = 256
VT_ROWS = V_HEAD_DIM + 2 * SUBLANES
ATTN_CHAIN = 1024
QKV_SUB = 256
POOL_HALO = 2 * SUBLANES
BAND_K = ROW_SUB + LANES

F32 = jnp.float32
BF16 = jnp.bfloat16

_ROT_SRC = np.array([i + 16 if (i % 32) < 16 else i - 16 for i in range(QK_ROPE_DIM)])
_ROT_SIGN = np.array([-1.0 if (i % 32) < 16 else 1.0 for i in range(QK_ROPE_DIM)], np.float32)


def _silu(x):
    return x * jax.nn.sigmoid(x)


def _dot(a, b):
    return jnp.dot(a, b, preferred_element_type=F32)


def _params(*sem):
    return pltpu.CompilerParams(dimension_semantics=sem, vmem_limit_bytes=VMEM_LIMIT)


def _resident(shape, layer=None):
    nd = len(shape)
    if layer is None:
        return pl.BlockSpec(shape, lambda *_: (0,) * nd, pipeline_mode=pl.Buffered(1))
    return pl.BlockSpec((None,) + tuple(shape), lambda *_: (layer,) + (0,) * nd, pipeline_mode=pl.Buffered(1))


def _ada_kernel(c_ref, w_ref, b_ref, o_ref):
    s = _silu(c_ref[...]).astype(BF16)
    o_ref[0] = _dot(s, w_ref[0].astype(BF16)) + b_ref[0]


def _ada_modulation(cc, w_ada, b_ada):
    tn = 512
    n3 = 3 * D_MODEL
    return pl.pallas_call(
        _ada_kernel,
        out_shape=jax.ShapeDtypeStruct((DEPTH, SUBLANES, n3), F32),
        grid=(DEPTH, n3 // tn),
        in_specs=[
            pl.BlockSpec((SUBLANES, D_MODEL), lambda l, j: (0, 0)),
            pl.BlockSpec((1, D_MODEL, tn), lambda l, j: (l, 0, j)),
            pl.BlockSpec((1, 1, tn), lambda l, j: (l, 0, j)),
        ],
        out_specs=pl.BlockSpec((1, SUBLANES, tn), lambda l, j: (l, 0, j)),
        compiler_params=_params("arbitrary", "arbitrary"),
        name="ada_modulation",
    )(cc, w_ada, b_ada.reshape(DEPTH, 1, n3))


def _in_proj_kernel(x_ref, mod_ref, nw_ref, w_ref, cs_ref, p_ref, uv_ref):
    mod = mod_ref[0]
    shift = mod[:, :D_MODEL]
    scale1 = (1.0 + mod[:, D_MODEL:2 * D_MODEL]) * nw_ref[...]
    tm = x_ref.shape[1]
    for r0 in range(0, tm, ROW_SUB):
        rows = slice(r0, min(r0 + ROW_SUB, tm))
        x = x_ref[0, rows, :]
        r = lax.rsqrt(jnp.mean(x * x, axis=-1, keepdims=True) + EPS)
        h = (x * r * scale1 + shift).astype(BF16)
        chunk = 512
        for c0 in range(0, D_IN_PACKED, chunk):
            c1 = min(c0 + chunk, D_IN_PACKED)
            pc = _dot(h, w_ref[:, c0:c1])
            p_ref[0, rows, c0:c1] = pc
            if c0 == COL_FNET:
                g = pc.astype(BF16)
                for hd in range(N_GROUPS):
                    cols = slice(hd * GROUP_CH, (hd + 1) * GROUP_CH)
                    uv = _dot(g[:, cols], cs_ref[...])
                    uv_ref[0, 0, rows, cols] = uv[:, :GROUP_CH].astype(BF16)
                    uv_ref[0, 1, rows, cols] = uv[:, GROUP_CH:].astype(BF16)


def _in_proj(xin, mod, norm_w, w_packed, cs_tab, *, layer, per_batch_mod, tm):
    b, n, _ = xin.shape
    mod_idx = (lambda bi, i: (bi, 0, 0)) if per_batch_mod else (lambda bi, i: (0, 0, 0))
    return pl.pallas_call(
        _in_proj_kernel,
        out_shape=(jax.ShapeDtypeStruct((b, n, D_IN_PACKED), F32),
                   jax.ShapeDtypeStruct((b, 2, n, POOL_WIDTH), BF16)),
        grid=(b, n // tm),
        in_specs=[
            pl.BlockSpec((1, tm, D_MODEL), lambda bi, i: (bi, i, 0)),
            pl.BlockSpec((1, 1, 3 * D_MODEL), mod_idx),
            _resident((1, D_MODEL)),
            _resident((D_MODEL, D_IN_PACKED), layer),
            _resident((GROUP_CH, 2 * GROUP_CH)),
        ],
        out_specs=(pl.BlockSpec((1, tm, D_IN_PACKED), lambda bi, i: (bi, i, 0)),
                   pl.BlockSpec((1, 2, tm, POOL_WIDTH), lambda bi, i: (bi, 0, i, 0))),
        compiler_params=_params("arbitrary", "arbitrary"),
        name="in_proj",
    )(xin, mod, norm_w, w_packed, cs_tab)


def _rms(x, w):
    return x * lax.rsqrt(jnp.mean(x * x, axis=-1, keepdims=True) + EPS) * w


def _rope(slab, t_cos, t_sin):
    return slab * t_cos + pltpu.roll(slab, LANES // 2, 1) * t_sin


def _normed_q_head(qh, cos, sin, hw):
    q_nope_w, q_rope_w, q_rot_w, q_pad = hw[0:1], hw[1:2], hw[2:3], hw[6:7]
    qn, qs = qh[:, :LANES], qh[:, LANES:]
    ss = jnp.sum(qn * qn + 0.5 * (qs * qs), axis=-1, keepdims=True)
    r = lax.rsqrt(ss * (1.0 / QK_HEAD_DIM) + EPS) * (SM_SCALE * LOG2_E)
    rope = _rope(qs, cos * q_rope_w, sin * q_rot_w) * r + q_pad
    return jnp.concatenate([qn * r * q_nope_w, rope], axis=-1).astype(BF16)


def _qkv_kernel(cq_ref, ckv_ref, kr_ref, qnw_ref, kvnw_ref, wuq_ref, wukv_ref, hw_ref,
                cos_ref, sin_ref, q_ref, k_ref, v_ref):
    hw = hw_ref[...]
    k_nope_w, k_rope_w, k_rot_w, k_pad = hw[3:4], hw[4:5], hw[5:6], hw[7:8]
    tm = cq_ref.shape[1]
    sub = min(QKV_SUB, tm)
    one_row = (lax.broadcasted_iota(jnp.int32, (VT_ROWS - V_HEAD_DIM, sub), 0) == 0).astype(BF16)
    inv_dim = 1.0 / QK_HEAD_DIM

    for r0 in range(0, tm, sub):
        rows = slice(r0, r0 + sub)
        cos = cos_ref[rows, :]
        sin = sin_ref[rows, :]
        cqn = _rms(cq_ref[0, rows, :], qnw_ref[...]).astype(BF16)
        for hd in range(MLA_HEADS):
            cols = slice(hd * HEAD_PAD, (hd + 1) * HEAD_PAD)
            q_ref[0, rows, cols] = _normed_q_head(_dot(cqn, wuq_ref[:, cols]), cos, sin, hw)

        ckvn = _rms(ckv_ref[0, rows, :], kvnw_ref[...]).astype(BF16)
        ks = kr_ref[0, rows, :]
        ss_rope = 0.5 * jnp.sum(ks * ks, axis=-1, keepdims=True)
        k_rope = _rope(ks, cos * k_rope_w, sin * k_rot_w)
        for hd in range(MLA_HEADS):
            kn = _dot(ckvn, wukv_ref[:, hd * LANES:(hd + 1) * LANES])
            ss = jnp.sum(kn * kn, axis=-1, keepdims=True) + ss_rope
            r = lax.rsqrt(ss * inv_dim + EPS)
            k_ref[0, rows, hd * HEAD_PAD:(hd + 1) * HEAD_PAD] = jnp.concatenate(
                [kn * r * k_nope_w, k_rope * r + k_pad], axis=-1).astype(BF16)
        v = _dot(ckvn, wukv_ref[:, MLA_HEADS * LANES:])
        for hd in range(MLA_HEADS):
            v_ref[0, hd, 0:V_HEAD_DIM, rows] = v[:, hd * LANES:(hd + 1) * LANES].T.astype(BF16)
            v_ref[0, hd, V_HEAD_DIM:, rows] = one_row


def _qkv(p, q_norm_w, kv_norm_w, wuq, wukv, head_w, cos_tab, sin_tab, *, layer, tm):
    b, n, _ = p.shape
    row = lambda bi, i: (i, 0)
    return pl.pallas_call(
        _qkv_kernel,
        out_shape=(jax.ShapeDtypeStruct((b, n, MLA_HEADS * HEAD_PAD), BF16),
                   jax.ShapeDtypeStruct((b, n, MLA_HEADS * HEAD_PAD), BF16),
                   jax.ShapeDtypeStruct((b, MLA_HEADS, VT_ROWS, n), BF16)),
        grid=(b, n // tm),
        in_specs=[
            pl.BlockSpec((1, tm, LORA_RANK), lambda bi, i: (bi, i, COL_CQ // LORA_RANK)),
            pl.BlockSpec((1, tm, LORA_RANK), lambda bi, i: (bi, i, COL_CKV // LORA_RANK)),
            pl.BlockSpec((1, tm, LANES), lambda bi, i: (bi, i, COL_KROPE // LANES)),
            _resident((1, LORA_RANK)),
            _resident((1, LORA_RANK)),
            _resident((LORA_RANK, MLA_HEADS * HEAD_PAD), layer),
            _resident((LORA_RANK, 2 * MLA_WIDTH), layer),
            _resident((SUBLANES, LANES)),
            pl.BlockSpec((tm, LANES), row),
            pl.BlockSpec((tm, LANES), row),
        ],
        out_specs=(pl.BlockSpec((1, tm, MLA_HEADS * HEAD_PAD), lambda bi, i: (bi, i, 0)),
                   pl.BlockSpec((1, tm, MLA_HEADS * HEAD_PAD), lambda bi, i: (bi, i, 0)),
                   pl.BlockSpec((1, MLA_HEADS, VT_ROWS, tm), lambda bi, i: (bi, 0, 0, i))),
        compiler_params=_params("arbitrary", "arbitrary"),
        name="qkv",
    )(p, p, p, q_norm_w, kv_norm_w, wuq, wukv, head_w, cos_tab, sin_tab)


def _scores_t(k, q):
    return lax.dot_general(k, q, (((1,), (1,)), ((), ())), preferred_element_type=F32)


def _attn_finish(acc, o_ref, rows=slice(None)):
    o_ref[0, rows, :] = (acc[:V_HEAD_DIM] / acc[V_HEAD_DIM:V_HEAD_DIM + 1]).T.astype(BF16)


def _attn_shifted_kernel(*refs, tk, n_lat):
    if n_lat:
        q_ref, kc_ref, vc_ref, kl_ref, vl_ref, o_ref = refs
    else:
        q_ref, kc_ref, vc_ref, o_ref = refs
    tq = q_ref.shape[1]
    cw = min(tq, ATTN_CHAIN)
    chains = [slice(r0, r0 + cw) for r0 in range(0, tq, cw)]
    qs = [q_ref[0, rows, :] for rows in chains]

    def tile(q, k, vt):
        return _dot(vt, jnp.exp2(_scores_t(k, q)).astype(BF16))

    accs = [tile(q, kc_ref[0], vc_ref[0, 0]) for q in qs]
    for j in range(n_lat // tk):
        k, vt = kl_ref[0, j * tk:(j + 1) * tk, :], vl_ref[0, 0, :, j * tk:(j + 1) * tk]
        accs = [acc + tile(q, k, vt) for q, acc in zip(qs, accs)]
    for rows, acc in zip(chains, accs):
        _attn_finish(acc, o_ref, rows)


def _attn_online_kernel(*refs, tk, n_lat):
    if n_lat:
        q_ref, kc_ref, vc_ref, kl_ref, vl_ref, o_ref = refs
    else:
        q_ref, kc_ref, vc_ref, o_ref = refs
    q = q_ref[0]
    tq = q.shape[0]

    def step(k, vt, m, acc):
        s = _scores_t(k, q)
        m_new = jnp.maximum(m, jnp.max(s, axis=0, keepdims=True))
        p = jnp.exp2(s - m_new)
        return m_new, jnp.exp2(m - m_new) * acc + _dot(vt, p.astype(BF16))

    m = jnp.full((1, tq), -jnp.inf, F32)
    acc = jnp.zeros((VT_ROWS, tq), F32)
    m, acc = step(kc_ref[0], vc_ref[0, 0], m, acc)
    if n_lat:
        def body(j, carry):
            j0 = pl.multiple_of(j * tk, tk)
            return step(kl_ref[0, pl.ds(j0, tk), :], vl_ref[0, 0, :, pl.ds(j0, tk)], *carry)
        m, acc = lax.fori_loop(0, n_lat // tk, body, (m, acc))
    _attn_finish(acc, o_ref)


def _attention(shift_is_safe, q, k_ctx, v_ctx, k_lat=None, v_lat=None, *, tq, tk=2048):
    b, nq, _ = q.shape
    n_ctx = k_ctx.shape[1]
    n_lat = 0 if k_lat is None else k_lat.shape[1]
    head = lambda bi, hd, i: (bi, 0, hd)
    head_t = lambda bi, hd, i: (bi, hd, 0, 0)
    in_specs = [
        pl.BlockSpec((1, tq, HEAD_PAD), lambda bi, hd, i: (bi, i, hd)),
        pl.BlockSpec((1, n_ctx, HEAD_PAD), head),
        pl.BlockSpec((1, 1, VT_ROWS, n_ctx), head_t),
    ]
    args = [q, k_ctx, v_ctx]
    if n_lat:
        in_specs += [pl.BlockSpec((1, n_lat, HEAD_PAD), head), pl.BlockSpec((1, 1, VT_ROWS, n_lat), head_t)]
        args += [k_lat, v_lat]

    def run(body, name):
        return pl.pallas_call(
            functools.partial(body, tk=tk, n_lat=n_lat),
            out_shape=jax.ShapeDtypeStruct((b, nq, MLA_WIDTH), BF16),
            grid=(b, MLA_HEADS, nq // tq),
            in_specs=in_specs,
            out_specs=pl.BlockSpec((1, tq, V_HEAD_DIM), lambda bi, hd, i: (bi, i, hd)),
            compiler_params=_params("arbitrary", "arbitrary", "arbitrary"),
            name=name,
        )

    return lax.cond(shift_is_safe,
                    lambda *a: run(_attn_shifted_kernel, "attention_shifted")(*a),
                    lambda *a: run(_attn_online_kernel, "attention_online")(*a),
                    *args)


def _pos_dft_kernel(tab_ref, uv_ref, o_ref, ue_ref, vo_ref, dp_ref, *, n):
    half = n // 2
    tr = min(ROW_SUB, half)
    tk = min(2 * ROW_SUB, half)
    inv = 1.0 / math.sqrt(n)
    u_mid = uv_ref[0, 0, half:half + 1, :].astype(F32) * inv
    anti = (lax.broadcasted_iota(jnp.int32, (tr, tr), 0) + lax.broadcasted_iota(jnp.int32, (tr, tr), 1)
            == tr).astype(BF16)
    first = lax.broadcasted_iota(jnp.int32, (tr, 1), 0) == 0

    def reverse_after(tile, head_row):
        return jnp.where(first, head_row.astype(F32), _dot(anti, tile))

    for t in range(half // tr):
        lo = slice(t * tr, (t + 1) * tr)
        hi = slice(n - (t + 1) * tr, n - t * tr)
        for c, (dst, sign) in enumerate(((ue_ref, 1.0), (vo_ref, -1.0))):
            head = uv_ref[0, c, n - t * tr:n - t * tr + 1, :] if t else jnp.zeros((1, POOL_WIDTH), BF16)
            partner = reverse_after(uv_ref[0, c, hi, :], head)
            dst[lo, :] = (uv_ref[0, c, lo, :].astype(F32) + sign * partner).astype(BF16)

    for i in range(half // tk):
        rows = slice(i * tk, (i + 1) * tk)
        k = i * tk + lax.broadcasted_iota(jnp.int32, (tk, 1), 0)
        p = _dot(tab_ref[rows, :half], ue_ref[...]) + (1 - 2 * (k % 2)).astype(F32) * u_mid
        q = _dot(tab_ref[rows, half:], vo_ref[...])
        o_ref[0, rows, :] = (p - q).astype(BF16)
        dp_ref[rows, :] = (p + q).astype(BF16)

    j = lax.broadcasted_iota(jnp.int32, (SUBLANES, half), 1)
    alt = ((1 - 2 * (j % 2)).astype(F32) * inv).astype(BF16)
    mid = (_dot(alt, ue_ref[...])[0:1] + u_mid).astype(BF16)

    for s in range(half // tr):
        head = dp_ref[half - s * tr:half - s * tr + 1, :] if s else mid
        tile = dp_ref[half - (s + 1) * tr:half - s * tr, :]
        o_ref[0, half + s * tr:half + (s + 1) * tr, :] = reverse_after(tile, head).astype(BF16)


def _pos_dft(tab, uv):
    b, _, n, _ = uv.shape
    half = n // 2
    return pl.pallas_call(
        functools.partial(_pos_dft_kernel, n=n),
        out_shape=jax.ShapeDtypeStruct((b, n, POOL_WIDTH), BF16),
        grid=(b,),
        in_specs=[_resident((half, n)),
                  pl.BlockSpec((1, 2, n, POOL_WIDTH), lambda bi: (bi, 0, 0, 0), pipeline_mode=pl.Buffered(1))],
        out_specs=pl.BlockSpec((1, n, POOL_WIDTH), lambda bi: (bi, 0, 0)),
        scratch_shapes=[pltpu.VMEM((half, POOL_WIDTH), BF16)] * 3,
        compiler_params=_params("arbitrary"),
        name="pos_dft",
    )(tab, uv)


def _out_proj_kernel(x_ref, mod_ref, pu_ref, prev_ref, next_ref, pg_ref, fg_ref, mg_ref, r_ref, at_ref,
                     band_ref, pw_ref, ps_ref, fw_ref, wo_ref, o_ref, ext_ref, *, tm, n):
    i = pl.program_id(1)
    ext_ref[0:POOL_HALO] = jnp.where(i > 0, prev_ref[0], 0.0).astype(BF16)
    ext_ref[POOL_HALO:POOL_HALO + tm] = pu_ref[0].astype(BF16)
    ext_ref[POOL_HALO + tm:2 * POOL_HALO + tm] = jnp.where(i < pl.num_programs(1) - 1, next_ref[0], 0.0).astype(BF16)
    ext_ref[2 * POOL_HALO + tm:] = jnp.zeros((BAND_K - ROW_SUB - 2 * POOL_HALO, POOL_WIDTH), BF16)
    gate = mod_ref[0][:, 2 * D_MODEL:]
    pair = 2 * GROUP_CH
    for r0 in range(0, tm, ROW_SUB):
        rows = slice(r0, r0 + ROW_SUB)
        t = i * tm + r0 + lax.broadcasted_iota(jnp.int32, (ROW_SUB, 1), 0)
        pooled = []
        for g, w in enumerate(POOL_WINDOWS):
            cols = slice(g * GROUP_CH, (g + 1) * GROUP_CH)
            lo = w // 2
            hi = w - lo - 1
            s = _dot(band_ref[g], ext_ref[r0:r0 + BAND_K, cols])
            cnt = (jnp.minimum(t + hi, n - 1) - jnp.maximum(t - lo, 0) + 1).astype(F32)
            pooled.append((s / cnt - pu_ref[0, rows, cols]).astype(BF16))
        parts = []
        for j in range(N_GROUPS // 2):
            cols = slice(j * pair, (j + 1) * pair)
            y = _dot(jnp.concatenate(pooled[2 * j:2 * j + 2], axis=-1), pw_ref[j]) * ps_ref[:, cols]
            parts.append((_silu(pg_ref[0, rows, cols]) * y).astype(BF16))
        for j in range(N_GROUPS // 2):
            cols = slice(j * pair, (j + 1) * pair)
            f = _dot(r_ref[0, rows, cols], fw_ref[j])
            parts.append((_silu(fg_ref[0, rows, cols]) * f).astype(BF16))
        parts.append((_silu(mg_ref[0, rows, :]) * at_ref[0, rows, :].astype(F32)).astype(BF16))
        mixed = jnp.concatenate(parts, axis=-1)
        chunk = 512
        for c0 in range(0, D_MODEL, chunk):
            y = _dot(mixed, wo_ref[:, c0:c0 + chunk])
            o_ref[0, rows, c0:c0 + chunk] = x_ref[0, rows, c0:c0 + chunk] + gate[:, c0:c0 + chunk] * y


def _out_proj(xin, mod, p, r, attn, pool_w, pool_scale, fnet_w, w_out, *, layer, per_batch_mod, tm):
    b, n, _ = xin.shape
    mod_idx = (lambda bi, i: (bi, 0, 0)) if per_batch_mod else (lambda bi, i: (0, 0, 0))
    hb = tm // POOL_HALO
    last = n // POOL_HALO - 1
    pcol = lambda width, off: pl.BlockSpec((1, tm, width), lambda bi, i: (bi, i, off // width))
    pair = 2 * GROUP_CH
    return pl.pallas_call(
        functools.partial(_out_proj_kernel, tm=tm, n=n),
        out_shape=jax.ShapeDtypeStruct((b, n, D_MODEL), F32),
        grid=(b, n // tm),
        in_specs=[
            pl.BlockSpec((1, tm, D_MODEL), lambda bi, i: (bi, i, 0)),
            pl.BlockSpec((1, 1, 3 * D_MODEL), mod_idx),
            pcol(POOL_WIDTH, COL_POOL),
            pl.BlockSpec((1, POOL_HALO, POOL_WIDTH), lambda bi, i: (bi, jnp.maximum(i * hb - 1, 0), 0)),
            pl.BlockSpec((1, POOL_HALO, POOL_WIDTH), lambda bi, i: (bi, jnp.minimum((i + 1) * hb, last), 0)),
            pcol(POOL_WIDTH, COL_POOL_GATE),
            pcol(POOL_WIDTH, COL_FNET_GATE),
            pcol(MLA_WIDTH, COL_MLA_GATE),
            pl.BlockSpec((1, tm, POOL_WIDTH), lambda bi, i: (bi, i, 0)),
            pl.BlockSpec((1, tm, MLA_WIDTH), lambda bi, i: (bi, i, 0)),
            _resident((N_GROUPS, ROW_SUB, BAND_K)),
            _resident((N_GROUPS // 2, pair, pair), layer),
            _resident((1, POOL_WIDTH)),
            _resident((N_GROUPS // 2, pair, pair), layer),
            _resident((D_MODEL, D_MODEL), layer),
        ],
        out_specs=pl.BlockSpec((1, tm, D_MODEL), lambda bi, i: (bi, i, 0)),
        scratch_shapes=[pltpu.VMEM((tm + BAND_K - ROW_SUB, POOL_WIDTH), BF16)],
        compiler_params=_params("arbitrary", "arbitrary"),
        name="out_proj",
    )(xin, mod, p, p, p, p, p, p, r, attn, _pool_band_table(), pool_w, pool_scale, fnet_w, w_out)


def _rope_tables(n):
    t = jnp.arange(n, dtype=jnp.int32)
    inv_freq = ROPE_THETA ** (-jnp.arange(N_FREQ_PER_AXIS, dtype=F32) / N_FREQ_PER_AXIS)
    ang_r = (t // GRID_W).astype(F32)[:, None] * inv_freq
    ang_c = (t % GRID_W).astype(F32)[:, None] * inv_freq
    ang = jnp.concatenate([ang_r, ang_r, ang_c, ang_c], axis=-1)
    pad = jnp.zeros((n, LANES - QK_ROPE_DIM), F32)
    return (jnp.concatenate([jnp.cos(ang), pad], axis=-1),
            jnp.concatenate([jnp.sin(ang) * _ROT_SIGN, pad], axis=-1))


def _identity_rope_tables(n):
    pad = jnp.zeros((n, LANES - QK_ROPE_DIM), F32)
    return jnp.concatenate([jnp.ones((n, QK_ROPE_DIM), F32), pad], axis=-1), jnp.zeros((n, LANES), F32)


def _channel_dft_table():
    k = np.arange(GROUP_CH)
    ang = 2.0 * np.pi * ((k[:, None] * k[None, :]) % GROUP_CH) / GROUP_CH
    tab = np.concatenate([np.cos(ang), np.sin(ang)], axis=1) / math.sqrt(GROUP_CH)
    return jnp.asarray(tab, F32).astype(BF16)


def _position_dft_table(n):
    half = n // 2
    g = 1
    while g * g < n:
        g *= 2
    period = n // g
    j = jnp.arange(half, dtype=jnp.int32)
    a = jnp.arange(half // g, dtype=jnp.int32)
    bb = jnp.arange(g, dtype=jnp.int32)
    ang_a = ((a[:, None] * j[None, :]) % period).astype(F32) * (2.0 * math.pi / period)
    ang_b = ((bb[:, None] * j[None, :]) % n).astype(F32) * (2.0 * math.pi / n)
    ca, sa = jnp.cos(ang_a)[:, None, :], jnp.sin(ang_a)[:, None, :]
    cb, sb = jnp.cos(ang_b)[None, :, :], jnp.sin(ang_b)[None, :, :]
    scale = 1.0 / math.sqrt(n)
    cos = ((ca * cb - sa * sb) * scale).reshape(half, half)
    sin = ((sa * cb + ca * sb) * scale).reshape(half, half)
    return jnp.concatenate([cos, sin], axis=1).astype(BF16)


def _pool_band_table():
    t = np.arange(ROW_SUB)[:, None]
    d = np.arange(BAND_K)[None, :] - POOL_HALO - t
    bands = [((d >= -(w // 2)) & (d <= w - w // 2 - 1)).astype(np.float32) for w in POOL_WINDOWS]
    return jnp.asarray(np.stack(bands), BF16)


def _pair_block_diag(w):
    even, odd = w[:, 0::2], w[:, 1::2]
    zero = jnp.zeros_like(even)
    return jnp.concatenate([jnp.concatenate([even, zero], axis=-1),
                            jnp.concatenate([zero, odd], axis=-1)], axis=-2).astype(BF16)


def _pack_w_in_kernel(w_ref, o_ref):
    w = w_ref[...]
    rows = w.shape[0]
    o_ref[:, :COL_MLA_GATE] = w[:, :COL_MLA_GATE].astype(BF16)
    o_ref[:, COL_MLA_GATE:COL_KROPE] = w[:, COL_MLA_GATE + QK_ROPE_DIM:].astype(BF16)
    x = w[:, COL_MLA_GATE:COL_MLA_GATE + LANES]
    lane = lax.broadcasted_iota(jnp.int32, (rows, LANES), 1)
    src = jnp.where(lane % 32 < 16, pltpu.roll(x, LANES - 16, 1), pltpu.roll(x, 16, 1))
    o_ref[:, COL_KROPE:] = jnp.where(lane < QK_ROPE_DIM, x, pltpu.roll(src, QK_ROPE_DIM, 1)).astype(BF16)


def _pack_w_in(w):
    depth, d, d_in = w.shape
    tr = 256
    packed = pl.pallas_call(
        _pack_w_in_kernel,
        out_shape=jax.ShapeDtypeStruct((depth * d, D_IN_PACKED), BF16),
        grid=(depth * d // tr,),
        in_specs=[pl.BlockSpec((tr, d_in), lambda i: (i, 0))],
        out_specs=pl.BlockSpec((tr, D_IN_PACKED), lambda i: (i, 0)),
        compiler_params=_params("arbitrary"),
        name="pack_w_in",
    )(w.reshape(depth * d, d_in))
    return packed.reshape(depth, d, D_IN_PACKED)


def _pack_w_uq(w):
    w = w.reshape(DEPTH, LORA_RANK, MLA_HEADS, QK_HEAD_DIM)
    rope = w[..., QK_NOPE_DIM:]
    return jnp.concatenate([w[..., :QK_NOPE_DIM], rope, rope[..., _ROT_SRC]], axis=-1).reshape(
        DEPTH, LORA_RANK, MLA_HEADS * HEAD_PAD).astype(BF16)


def _pack_w_ukv(w):
    w = w.reshape(DEPTH, LORA_RANK, MLA_HEADS, QK_NOPE_DIM + V_HEAD_DIM)
    return jnp.concatenate([w[..., :QK_NOPE_DIM].reshape(DEPTH, LORA_RANK, -1),
                            w[..., QK_NOPE_DIM:].reshape(DEPTH, LORA_RANK, -1)], axis=-1).astype(BF16)


def _pack_head_w(qw, kw):
    pad = jnp.zeros((LANES - QK_ROPE_DIM,), F32)
    rows = []
    for w in (qw, kw):
        rope = w[QK_NOPE_DIM:]
        rows += [w[:QK_NOPE_DIM], jnp.concatenate([rope, pad]), jnp.concatenate([rope[_ROT_SRC], pad])]
    shift = QK_HEAD_DIM * SM_SCALE * LOG2_E * jnp.max(jnp.abs(qw)) * jnp.max(jnp.abs(kw))
    spare = (np.arange(LANES) == QK_ROPE_DIM).astype(np.float32)
    rows += [-shift * spare, jnp.asarray(spare)]
    return jnp.stack(rows), shift <= MAX_SAFE_SHIFT


def kernel(x, c, ctx, c_ctx, norm_w, w_ada, b_ada, w_in, pool_w, pool_scale, fnet_w, q_norm_w, w_uq,
           kv_norm_w, w_ukv, q_head_norm_w, k_head_norm_w, w_out):
    batch, seq, _ = x.shape
    n_ctx = ctx.shape[1]
    cc = jnp.concatenate([c, c_ctx[None, :], jnp.zeros((SUBLANES - batch - 1, D_MODEL), F32)], axis=0)
    mod_all = _ada_modulation(cc, w_ada, b_ada)

    cs_tab = _channel_dft_table()
    rope_lat = _rope_tables(seq)
    rope_ctx = _identity_rope_tables(n_ctx)
    dft_lat = _position_dft_table(seq)
    dft_ctx = _position_dft_table(n_ctx)
    tm_lat, tm_qkv, tm_ctx = 512, 512, n_ctx

    w_in_p, wuq, wukv = _pack_w_in(w_in), _pack_w_uq(w_uq), _pack_w_ukv(w_ukv)
    pw, fw, wo = _pair_block_diag(pool_w), _pair_block_diag(fnet_w), w_out.astype(BF16)

    for l in range(DEPTH):
        mod_lat = mod_all[l, :batch][:, None, :]
        mod_ctx = mod_all[l, batch:batch + 1][:, None, :]
        nw = norm_w[l][None, :]
        head_w, shift_is_safe = _pack_head_w(q_head_norm_w[l], k_head_norm_w[l])
        qnw, kvnw = q_norm_w[l][None, :], kv_norm_w[l][None, :]
        ps = pool_scale[l][None, :]
        ctx_out = l < DEPTH - 1

        p_c, uv_c = _in_proj(ctx, mod_ctx, nw, w_in_p, cs_tab, layer=l, per_batch_mod=False, tm=tm_ctx)
        q_c, k_c, v_c = _qkv(p_c, qnw, kvnw, wuq, wukv, head_w, *rope_ctx, layer=l, tm=tm_ctx)
        p_l, uv_l = _in_proj(x, mod_lat, nw, w_in_p, cs_tab, layer=l, per_batch_mod=True, tm=tm_lat)
        q_l, k_l, v_l = _qkv(p_l, qnw, kvnw, wuq, wukv, head_w, *rope_lat, layer=l, tm=tm_qkv)

        attn_l = _attention(shift_is_safe, q_l, k_c, v_c, k_l, v_l, tq=2048)
        r_l = _pos_dft(dft_lat, uv_l)
        x_new = _out_proj(x, mod_lat, p_l, r_l, attn_l, pw, ps, fw, wo, layer=l, per_batch_mod=True, tm=tm_lat)
        if ctx_out:
            attn_c = _attention(shift_is_safe, q_c, k_c, v_c, tq=n_ctx)
            r_c = _pos_dft(dft_ctx, uv_c)
            ctx = _out_proj(ctx, mod_ctx, p_c, r_c, attn_c, pw, ps, fw, wo, layer=l, per_batch_mod=False,
                            tm=tm_ctx)
        x = x_new
    return x
```

```python
import functools
import math

import numpy as np
import jax
import jax.numpy as jnp
from jax import lax
from jax.experimental import pallas as pl
from jax.experimental.pallas import tpu as pltpu

D_MODEL = 2048
DEPTH = 2
GRID_W = 64
POOL_WIDTH = 512
POOL_WINDOWS = (2, 4, 8, 16)
GROUP_CH = 128
N_GROUPS = 4
MLA_HEADS = 8
QK_NOPE_DIM = 128
QK_ROPE_DIM = 64
QK_HEAD_DIM = QK_NOPE_DIM + QK_ROPE_DIM
V_HEAD_DIM = 128
MLA_WIDTH = MLA_HEADS * V_HEAD_DIM
LORA_RANK = 512
N_FREQ_PER_AXIS = QK_ROPE_DIM // 4
ROPE_THETA = 10000.0
EPS = 1e-6
SM_SCALE = QK_HEAD_DIM ** -0.5
LOG2_E = math.log2(math.e)
MAX_SAFE_SHIFT = 60.0

COL_POOL = 0
COL_POOL_GATE = 512
COL_FNET = 1024
COL_FNET_GATE = 1536
COL_CQ = 2048
COL_CKV = 2560
COL_MLA_GATE = 3072
COL_KROPE = 4096
D_IN_PACKED = 4224
HEAD_PAD = 256

LANES = 128
SUBLANES = 8
VMEM_LIMIT = 56 * 1024 * 1024
ROW_SUB = 256
VT_ROWS = V_HEAD_DIM + 2 * SUBLANES
ATTN_CHAIN = 1024
QKV_SUB = 256
POOL_HALO = 2 * SUBLANES
BAND_K = ROW_SUB + LANES

F32 = jnp.float32
BF16 = jnp.bfloat16

_ROT_SRC = np.array([i + 16 if (i % 32) < 16 else i - 16 for i in range(QK_ROPE_DIM)])
_ROT_SIGN = np.array([-1.0 if (i % 32) < 16 else 1.0 for i in range(QK_ROPE_DIM)], np.float32)


def _silu(x):
    return x * jax.nn.sigmoid(x)


def _dot(a, b):
    return jnp.dot(a, b, preferred_element_type=F32)


def _params(*sem):
    return pltpu.CompilerParams(dimension_semantics=sem, vmem_limit_bytes=VMEM_LIMIT)


def _resident(shape, layer=None):
    nd = len(shape)
    if layer is None:
        return pl.BlockSpec(shape, lambda *_: (0,) * nd, pipeline_mode=pl.Buffered(1))
    return pl.BlockSpec((None,) + tuple(shape), lambda *_: (layer,) + (0,) * nd, pipeline_mode=pl.Buffered(1))


def _ada_kernel(c_ref, w_ref, b_ref, o_ref):
    s = _silu(c_ref[...]).astype(BF16)
    o_ref[0] = _dot(s, w_ref[0].astype(BF16)) + b_ref[0]


def _ada_modulation(cc, w_ada, b_ada):
    tn = 512
    n3 = 3 * D_MODEL
    return pl.pallas_call(
        _ada_kernel,
        out_shape=jax.ShapeDtypeStruct((DEPTH, SUBLANES, n3), F32),
        grid=(DEPTH, n3 // tn),
        in_specs=[
            pl.BlockSpec((SUBLANES, D_MODEL), lambda l, j: (0, 0)),
            pl.BlockSpec((1, D_MODEL, tn), lambda l, j: (l, 0, j)),
            pl.BlockSpec((1, 1, tn), lambda l, j: (l, 0, j)),
        ],
        out_specs=pl.BlockSpec((1, SUBLANES, tn), lambda l, j: (l, 0, j)),
        compiler_params=_params("arbitrary", "arbitrary"),
        name="ada_modulation",
    )(cc, w_ada, b_ada.reshape(DEPTH, 1, n3))


def _in_proj_kernel(x_ref, mod_ref, nw_ref, w_ref, cs_ref, p_ref, uv_ref):
    mod = mod_ref[0]
    shift = mod[:, :D_MODEL]
    scale1 = (1.0 + mod[:, D_MODEL:2 * D_MODEL]) * nw_ref[...]
    tm = x_ref.shape[1]
    for r0 in range(0, tm, ROW_SUB):
        rows = slice(r0, min(r0 + ROW_SUB, tm))
        x = x_ref[0, rows, :]
        r = lax.rsqrt(jnp.mean(x * x, axis=-1, keepdims=True) + EPS)
        h = (x * r * scale1 + shift).astype(BF16)
        chunk = 512
        for c0 in range(0, D_IN_PACKED, chunk):
            c1 = min(c0 + chunk, D_IN_PACKED)
            pc = _dot(h, w_ref[:, c0:c1])
            p_ref[0, rows, c0:c1] = pc
            if c0 == COL_FNET:
                g = pc.astype(BF16)
                for hd in range(N_GROUPS):
                    cols = slice(hd * GROUP_CH, (hd + 1) * GROUP_CH)
                    uv = _dot(g[:, cols], cs_ref[...])
                    uv_ref[0, 0, rows, cols] = uv[:, :GROUP_CH].astype(BF16)
                    uv_ref[0, 1, rows, cols] = uv[:, GROUP_CH:].astype(BF16)


def _in_proj(xin, mod, norm_w, w_packed, cs_tab, *, layer, per_batch_mod, tm):
    b, n, _ = xin.shape
    mod_idx = (lambda bi, i: (bi, 0, 0)) if per_batch_mod else (lambda bi, i: (0, 0, 0))
    return pl.pallas_call(
        _in_proj_kernel,
        out_shape=(jax.ShapeDtypeStruct((b, n, D_IN_PACKED), F32),
                   jax.ShapeDtypeStruct((b, 2, n, POOL_WIDTH), BF16)),
        grid=(b, n // tm),
        in_specs=[
            pl.BlockSpec((1, tm, D_MODEL), lambda bi, i: (bi, i, 0)),
            pl.BlockSpec((1, 1, 3 * D_MODEL), mod_idx),
            _resident((1, D_MODEL)),
            _resident((D_MODEL, D_IN_PACKED), layer),
            _resident((GROUP_CH, 2 * GROUP_CH)),
        ],
        out_specs=(pl.BlockSpec((1, tm, D_IN_PACKED), lambda bi, i: (bi, i, 0)),
                   pl.BlockSpec((1, 2, tm, POOL_WIDTH), lambda bi, i: (bi, 0, i, 0))),
        compiler_params=_params("arbitrary", "arbitrary"),
        name="in_proj",
    )(xin, mod, norm_w, w_packed, cs_tab)


def _rms(x, w):
    return x * lax.rsqrt(jnp.mean(x * x, axis=-1, keepdims=True) + EPS) * w


def _rope(slab, t_cos, t_sin):
    return slab * t_cos + pltpu.roll(slab, LANES // 2, 1) * t_sin


def _normed_q_head(qh, cos, sin, hw):
    q_nope_w, q_rope_w, q_rot_w, q_pad = hw[0:1], hw[1:2], hw[2:3], hw[6:7]
    qn, qs = qh[:, :LANES], qh[:, LANES:]
    ss = jnp.sum(qn * qn + 0.5 * (qs * qs), axis=-1, keepdims=True)
    r = lax.rsqrt(ss * (1.0 / QK_HEAD_DIM) + EPS) * (SM_SCALE * LOG2_E)
    rope = _rope(qs, cos * q_rope_w, sin * q_rot_w) * r + q_pad
    return jnp.concatenate([qn * r * q_nope_w, rope], axis=-1).astype(BF16)


def _qkv_kernel(cq_ref, ckv_ref, kr_ref, qnw_ref, kvnw_ref, wuq_ref, wukv_ref, hw_ref,
                cos_ref, sin_ref, q_ref, k_ref, v_ref):
    hw = hw_ref[...]
    k_nope_w, k_rope_w, k_rot_w, k_pad = hw[3:4], hw[4:5], hw[5:6], hw[7:8]
    tm = cq_ref.shape[1]
    sub = min(QKV_SUB, tm)
    one_row = (lax.broadcasted_iota(jnp.int32, (VT_ROWS - V_HEAD_DIM, sub), 0) == 0).astype(BF16)
    inv_dim = 1.0 / QK_HEAD_DIM

    for r0 in range(0, tm, sub):
        rows = slice(r0, r0 + sub)
        cos = cos_ref[rows, :]
        sin = sin_ref[rows, :]
        cqn = _rms(cq_ref[0, rows, :], qnw_ref[...]).astype(BF16)
        for hd in range(MLA_HEADS):
            cols = slice(hd * HEAD_PAD, (hd + 1) * HEAD_PAD)
            q_ref[0, rows, cols] = _normed_q_head(_dot(cqn, wuq_ref[:, cols]), cos, sin, hw)

        ckvn = _rms(ckv_ref[0, rows, :], kvnw_ref[...]).astype(BF16)
        ks = kr_ref[0, rows, :]
        ss_rope = 0.5 * jnp.sum(ks * ks, axis=-1, keepdims=True)
        k_rope = _rope(ks, cos * k_rope_w, sin * k_rot_w)
        for hd in range(MLA_HEADS):
            kn = _dot(ckvn, wukv_ref[:, hd * LANES:(hd + 1) * LANES])
            ss = jnp.sum(kn * kn, axis=-1, keepdims=True) + ss_rope
            r = lax.rsqrt(ss * inv_dim + EPS)
            k_ref[0, rows, hd * HEAD_PAD:(hd + 1) * HEAD_PAD] = jnp.concatenate(
                [kn * r * k_nope_w, k_rope * r + k_pad], axis=-1).astype(BF16)
        v = _dot(ckvn, wukv_ref[:, MLA_HEADS * LANES:])
        for hd in range(MLA_HEADS):
            v_ref[0, hd, 0:V_HEAD_DIM, rows] = v[:, hd * LANES:(hd + 1) * LANES].T.astype(BF16)
            v_ref[0, hd, V_HEAD_DIM:, rows] = one_row


def _qkv(p, q_norm_w, kv_norm_w, wuq, wukv, head_w, cos_tab, sin_tab, *, layer, tm):
    b, n, _ = p.shape
    row = lambda bi, i: (i, 0)
    return pl.pallas_call(
        _qkv_kernel,
        out_shape=(jax.ShapeDtypeStruct((b, n, MLA_HEADS * HEAD_PAD), BF16),
                   jax.ShapeDtypeStruct((b, n, MLA_HEADS * HEAD_PAD), BF16),
                   jax.ShapeDtypeStruct((b, MLA_HEADS, VT_ROWS, n), BF16)),
        grid=(b, n // tm),
        in_specs=[
            pl.BlockSpec((1, tm, LORA_RANK), lambda bi, i: (bi, i, COL_CQ // LORA_RANK)),
            pl.BlockSpec((1, tm, LORA_RANK), lambda bi, i: (bi, i, COL_CKV // LORA_RANK)),
            pl.BlockSpec((1, tm, LANES), lambda bi, i: (bi, i, COL_KROPE // LANES)),
            _resident((1, LORA_RANK)),
            _resident((1, LORA_RANK)),
            _resident((LORA_RANK, MLA_HEADS * HEAD_PAD), layer),
            _resident((LORA_RANK, 2 * MLA_WIDTH), layer),
            _resident((SUBLANES, LANES)),
            pl.BlockSpec((tm, LANES), row),
            pl.BlockSpec((tm, LANES), row),
        ],
        out_specs=(pl.BlockSpec((1, tm, MLA_HEADS * HEAD_PAD), lambda bi, i: (bi, i, 0)),
                   pl.BlockSpec((1, tm, MLA_HEADS * HEAD_PAD), lambda bi, i: (bi, i, 0)),
                   pl.BlockSpec((1, MLA_HEADS, VT_ROWS, tm), lambda bi, i: (bi, 0, 0, i))),
        compiler_params=_params("arbitrary", "arbitrary"),
        name="qkv",
    )(p, p, p, q_norm_w, kv_norm_w, wuq, wukv, head_w, cos_tab, sin_tab)


def _scores_t(k, q):
    return lax.dot_general(k, q, (((1,), (1,)), ((), ())), preferred_element_type=F32)


def _attn_finish(acc, o_ref, rows=slice(None)):
    o_ref[0, rows, :] = (acc[:V_HEAD_DIM] / acc[V_HEAD_DIM:V_HEAD_DIM + 1]).T.astype(BF16)


def _attn_shifted_kernel(*refs, tk, n_lat):
    if n_lat:
        q_ref, kc_ref, vc_ref, kl_ref, vl_ref, o_ref = refs
    else:
        q_ref, kc_ref, vc_ref, o_ref = refs
    tq = q_ref.shape[1]
    cw = min(tq, ATTN_CHAIN)
    chains = [slice(r0, r0 + cw) for r0 in range(0, tq, cw)]
    qs = [q_ref[0, rows, :] for rows in chains]

    def tile(q, k, vt):
        return _dot(vt, jnp.exp2(_scores_t(k, q)).astype(BF16))

    accs = [tile(q, kc_ref[0], vc_ref[0, 0]) for q in qs]
    for j in range(n_lat // tk):
        k, vt = kl_ref[0, j * tk:(j + 1) * tk, :], vl_ref[0, 0, :, j * tk:(j + 1) * tk]
        accs = [acc + tile(q, k, vt) for q, acc in zip(qs, accs)]
    for rows, acc in zip(chains, accs):
        _attn_finish(acc, o_ref, rows)


def _attn_online_kernel(*refs, tk, n_lat):
    if n_lat:
        q_ref, kc_ref, vc_ref, kl_ref, vl_ref, o_ref = refs
    else:
        q_ref, kc_ref, vc_ref, o_ref = refs
    q = q_ref[0]
    tq = q.shape[0]

    def step(k, vt, m, acc):
        s = _scores_t(k, q)
        m_new = jnp.maximum(m, jnp.max(s, axis=0, keepdims=True))
        p = jnp.exp2(s - m_new)
        return m_new, jnp.exp2(m - m_new) * acc + _dot(vt, p.astype(BF16))

    m = jnp.full((1, tq), -jnp.inf, F32)
    acc = jnp.zeros((VT_ROWS, tq), F32)
    m, acc = step(kc_ref[0], vc_ref[0, 0], m, acc)
    if n_lat:
        def body(j, carry):
            j0 = pl.multiple_of(j * tk, tk)
            return step(kl_ref[0, pl.ds(j0, tk), :], vl_ref[0, 0, :, pl.ds(j0, tk)], *carry)
        m, acc = lax.fori_loop(0, n_lat // tk, body, (m, acc))
    _attn_finish(acc, o_ref)


def _attention(shift_is_safe, q, k_ctx, v_ctx, k_lat=None, v_lat=None, *, tq, tk=4096):
    b, nq, _ = q.shape
    n_ctx = k_ctx.shape[1]
    n_lat = 0 if k_lat is None else k_lat.shape[1]
    head = lambda bi, hd, i: (bi, 0, hd)
    head_t = lambda bi, hd, i: (bi, hd, 0, 0)
    in_specs = [
        pl.BlockSpec((1, tq, HEAD_PAD), lambda bi, hd, i: (bi, i, hd)),
        pl.BlockSpec((1, n_ctx, HEAD_PAD), head),
        pl.BlockSpec((1, 1, VT_ROWS, n_ctx), head_t),
    ]
    args = [q, k_ctx, v_ctx]
    if n_lat:
        in_specs += [pl.BlockSpec((1, n_lat, HEAD_PAD), head), pl.BlockSpec((1, 1, VT_ROWS, n_lat), head_t)]
        args += [k_lat, v_lat]

    def run(body, name):
        return pl.pallas_call(
            functools.partial(body, tk=tk, n_lat=n_lat),
            out_shape=jax.ShapeDtypeStruct((b, nq, MLA_WIDTH), BF16),
            grid=(b, MLA_HEADS, nq // tq),
            in_specs=in_specs,
            out_specs=pl.BlockSpec((1, tq, V_HEAD_DIM), lambda bi, hd, i: (bi, i, hd)),
            compiler_params=_params("arbitrary", "arbitrary", "arbitrary"),
            name=name,
        )

    return lax.cond(shift_is_safe,
                    lambda *a: run(_attn_shifted_kernel, "attention_shifted")(*a),
                    lambda *a: run(_attn_online_kernel, "attention_online")(*a),
                    *args)


def _pos_dft_kernel(tab_ref, uv_ref, o_ref, ue_ref, vo_ref, dp_ref, *, n):
    half = n // 2
    tr = min(ROW_SUB, half)
    tk = min(2 * ROW_SUB, half)
    inv = 1.0 / math.sqrt(n)
    u_mid = uv_ref[0, 0, half:half + 1, :].astype(F32) * inv
    anti = (lax.broadcasted_iota(jnp.int32, (tr, tr), 0) + lax.broadcasted_iota(jnp.int32, (tr, tr), 1)
            == tr).astype(BF16)
    first = lax.broadcasted_iota(jnp.int32, (tr, 1), 0) == 0

    def reverse_after(tile, head_row):
        return jnp.where(first, head_row.astype(F32), _dot(anti, tile))

    for t in range(half // tr):
        lo = slice(t * tr, (t + 1) * tr)
        hi = slice(n - (t + 1) * tr, n - t * tr)
        for c, (dst, sign) in enumerate(((ue_ref, 1.0), (vo_ref, -1.0))):
            head = uv_ref[0, c, n - t * tr:n - t * tr + 1, :] if t else jnp.zeros((1, POOL_WIDTH), BF16)
            partner = reverse_after(uv_ref[0, c, hi, :], head)
            dst[lo, :] = (uv_ref[0, c, lo, :].astype(F32) + sign * partner).astype(BF16)

    for i in range(half // tk):
        rows = slice(i * tk, (i + 1) * tk)
        k = i * tk + lax.broadcasted_iota(jnp.int32, (tk, 1), 0)
        p = _dot(tab_ref[rows, :half], ue_ref[...]) + (1 - 2 * (k % 2)).astype(F32) * u_mid
        q = _dot(tab_ref[rows, half:], vo_ref[...])
        o_ref[0, rows, :] = (p - q).astype(BF16)
        dp_ref[rows, :] = (p + q).astype(BF16)

    j = lax.broadcasted_iota(jnp.int32, (SUBLANES, half), 1)
    alt = ((1 - 2 * (j % 2)).astype(F32) * inv).astype(BF16)
    mid = (_dot(alt, ue_ref[...])[0:1] + u_mid).astype(BF16)

    for s in range(half // tr):
        head = dp_ref[half - s * tr:half - s * tr + 1, :] if s else mid
        tile = dp_ref[half - (s + 1) * tr:half - s * tr, :]
        o_ref[0, half + s * tr:half + (s + 1) * tr, :] = reverse_after(tile, head).astype(BF16)


def _pos_dft(tab, uv):
    b, _, n, _ = uv.shape
    half = n // 2
    return pl.pallas_call(
        functools.partial(_pos_dft_kernel, n=n),
        out_shape=jax.ShapeDtypeStruct((b, n, POOL_WIDTH), BF16),
        grid=(b,),
        in_specs=[_resident((half, n)),
                  pl.BlockSpec((1, 2, n, POOL_WIDTH), lambda bi: (bi, 0, 0, 0), pipeline_mode=pl.Buffered(1))],
        out_specs=pl.BlockSpec((1, n, POOL_WIDTH), lambda bi: (bi, 0, 0)),
        scratch_shapes=[pltpu.VMEM((half, POOL_WIDTH), BF16)] * 3,
        compiler_params=_params("arbitrary"),
        name="pos_dft",
    )(tab, uv)


def _out_proj_kernel(x_ref, mod_ref, pu_ref, prev_ref, next_ref, pg_ref, fg_ref, mg_ref, r_ref, at_ref,
                     band_ref, pw_ref, ps_ref, fw_ref, wo_ref, o_ref, ext_ref, *, tm, n):
    i = pl.program_id(1)
    ext_ref[0:POOL_HALO] = jnp.where(i > 0, prev_ref[0], 0.0).astype(BF16)
    ext_ref[POOL_HALO:POOL_HALO + tm] = pu_ref[0].astype(BF16)
    ext_ref[POOL_HALO + tm:2 * POOL_HALO + tm] = jnp.where(i < pl.num_programs(1) - 1, next_ref[0], 0.0).astype(BF16)
    ext_ref[2 * POOL_HALO + tm:] = jnp.zeros((BAND_K - ROW_SUB - 2 * POOL_HALO, POOL_WIDTH), BF16)
    gate = mod_ref[0][:, 2 * D_MODEL:]
    pair = 2 * GROUP_CH
    for r0 in range(0, tm, ROW_SUB):
        rows = slice(r0, r0 + ROW_SUB)
        t = i * tm + r0 + lax.broadcasted_iota(jnp.int32, (ROW_SUB, 1), 0)
        pooled = []
        for g, w in enumerate(POOL_WINDOWS):
            cols = slice(g * GROUP_CH, (g + 1) * GROUP_CH)
            lo = w // 2
            hi = w - lo - 1
            s = _dot(band_ref[g], ext_ref[r0:r0 + BAND_K, cols])
            cnt = (jnp.minimum(t + hi, n - 1) - jnp.maximum(t - lo, 0) + 1).astype(F32)
            pooled.append((s / cnt - pu_ref[0, rows, cols]).astype(BF16))
        parts = []
        for j in range(N_GROUPS // 2):
            cols = slice(j * pair, (j + 1) * pair)
            y = _dot(jnp.concatenate(pooled[2 * j:2 * j + 2], axis=-1), pw_ref[j]) * ps_ref[:, cols]
            parts.append((_silu(pg_ref[0, rows, cols]) * y).astype(BF16))
        for j in range(N_GROUPS // 2):
            cols = slice(j * pair, (j + 1) * pair)
            f = _dot(r_ref[0, rows, cols], fw_ref[j])
            parts.append((_silu(fg_ref[0, rows, cols]) * f).astype(BF16))
        parts.append((_silu(mg_ref[0, rows, :]) * at_ref[0, rows, :].astype(F32)).astype(BF16))
        mixed = jnp.concatenate(parts, axis=-1)
        chunk = 512
        for c0 in range(0, D_MODEL, chunk):
            y = _dot(mixed, wo_ref[:, c0:c0 + chunk])
            o_ref[0, rows, c0:c0 + chunk] = x_ref[0, rows, c0:c0 + chunk] + gate[:, c0:c0 + chunk] * y


def _out_proj(xin, mod, p, r, attn, pool_w, pool_scale, fnet_w, w_out, *, layer, per_batch_mod, tm):
    b, n, _ = xin.shape
    mod_idx = (lambda bi, i: (bi, 0, 0)) if per_batch_mod else (lambda bi, i: (0, 0, 0))
    hb = tm // POOL_HALO
    last = n // POOL_HALO - 1
    pcol = lambda width, off: pl.BlockSpec((1, tm, width), lambda bi, i: (bi, i, off // width))
    pair = 2 * GROUP_CH
    return pl.pallas_call(
        functools.partial(_out_proj_kernel, tm=tm, n=n),
        out_shape=jax.ShapeDtypeStruct((b, n, D_MODEL), F32),
        grid=(b, n // tm),
        in_specs=[
            pl.BlockSpec((1, tm, D_MODEL), lambda bi, i: (bi, i, 0)),
            pl.BlockSpec((1, 1, 3 * D_MODEL), mod_idx),
            pcol(POOL_WIDTH, COL_POOL),
            pl.BlockSpec((1, POOL_HALO, POOL_WIDTH), lambda bi, i: (bi, jnp.maximum(i * hb - 1, 0), 0)),
            pl.BlockSpec((1, POOL_HALO, POOL_WIDTH), lambda bi, i: (bi, jnp.minimum((i + 1) * hb, last), 0)),
            pcol(POOL_WIDTH, COL_POOL_GATE),
            pcol(POOL_WIDTH, COL_FNET_GATE),
            pcol(MLA_WIDTH, COL_MLA_GATE),
            pl.BlockSpec((1, tm, POOL_WIDTH), lambda bi, i: (bi, i, 0)),
            pl.BlockSpec((1, tm, MLA_WIDTH), lambda bi, i: (bi, i, 0)),
            _resident((N_GROUPS, ROW_SUB, BAND_K)),
            _resident((N_GROUPS // 2, pair, pair), layer),
            _resident((1, POOL_WIDTH)),
            _resident((N_GROUPS // 2, pair, pair), layer),
            _resident((D_MODEL, D_MODEL), layer),
        ],
        out_specs=pl.BlockSpec((1, tm, D_MODEL), lambda bi, i: (bi, i, 0)),
        scratch_shapes=[pltpu.VMEM((tm + BAND_K - ROW_SUB, POOL_WIDTH), BF16)],
        compiler_params=_params("arbitrary", "arbitrary"),
        name="out_proj",
    )(xin, mod, p, p, p, p, p, p, r, attn, _pool_band_table(), pool_w, pool_scale, fnet_w, w_out)


def _rope_tables(n):
    t = jnp.arange(n, dtype=jnp.int32)
    inv_freq = ROPE_THETA ** (-jnp.arange(N_FREQ_PER_AXIS, dtype=F32) / N_FREQ_PER_AXIS)
    ang_r = (t // GRID_W).astype(F32)[:, None] * inv_freq
    ang_c = (t % GRID_W).astype(F32)[:, None] * inv_freq
    ang = jnp.concatenate([ang_r, ang_r, ang_c, ang_c], axis=-1)
    pad = jnp.zeros((n, LANES - QK_ROPE_DIM), F32)
    return (jnp.concatenate([jnp.cos(ang), pad], axis=-1),
            jnp.concatenate([jnp.sin(ang) * _ROT_SIGN, pad], axis=-1))


def _identity_rope_tables(n):
    pad = jnp.zeros((n, LANES - QK_ROPE_DIM), F32)
    return jnp.concatenate([jnp.ones((n, QK_ROPE_DIM), F32), pad], axis=-1), jnp.zeros((n, LANES), F32)


def _channel_dft_table():
    k = np.arange(GROUP_CH)
    ang = 2.0 * np.pi * ((k[:, None] * k[None, :]) % GROUP_CH) / GROUP_CH
    tab = np.concatenate([np.cos(ang), np.sin(ang)], axis=1) / math.sqrt(GROUP_CH)
    return jnp.asarray(tab, F32).astype(BF16)


def _position_dft_table(n):
    half = n // 2
    g = 1
    while g * g < n:
        g *= 2
    period = n // g
    j = jnp.arange(half, dtype=jnp.int32)
    a = jnp.arange(half // g, dtype=jnp.int32)
    bb = jnp.arange(g, dtype=jnp.int32)
    ang_a = ((a[:, None] * j[None, :]) % period).astype(F32) * (2.0 * math.pi / period)
    ang_b = ((bb[:, None] * j[None, :]) % n).astype(F32) * (2.0 * math.pi / n)
    ca, sa = jnp.cos(ang_a)[:, None, :], jnp.sin(ang_a)[:, None, :]
    cb, sb = jnp.cos(ang_b)[None, :, :], jnp.sin(ang_b)[None, :, :]
    scale = 1.0 / math.sqrt(n)
    cos = ((ca * cb - sa * sb) * scale).reshape(half, half)
    sin = ((sa * cb + ca * sb) * scale).reshape(half, half)
    return jnp.concatenate([cos, sin], axis=1).astype(BF16)


def _pool_band_table():
    t = np.arange(ROW_SUB)[:, None]
    d = np.arange(BAND_K)[None, :] - POOL_HALO - t
    bands = [((d >= -(w // 2)) & (d <= w - w // 2 - 1)).astype(np.float32) for w in POOL_WINDOWS]
    return jnp.asarray(np.stack(bands), BF16)


def _pair_block_diag(w):
    even, odd = w[:, 0::2], w[:, 1::2]
    zero = jnp.zeros_like(even)
    return jnp.concatenate([jnp.concatenate([even, zero], axis=-1),
                            jnp.concatenate([zero, odd], axis=-1)], axis=-2).astype(BF16)


def _pack_w_in_kernel(w_ref, o_ref):
    w = w_ref[...]
    rows = w.shape[0]
    o_ref[:, :COL_MLA_GATE] = w[:, :COL_MLA_GATE].astype(BF16)
    o_ref[:, COL_MLA_GATE:COL_KROPE] = w[:, COL_MLA_GATE + QK_ROPE_DIM:].astype(BF16)
    x = w[:, COL_MLA_GATE:COL_MLA_GATE + LANES]
    lane = lax.broadcasted_iota(jnp.int32, (rows, LANES), 1)
    src = jnp.where(lane % 32 < 16, pltpu.roll(x, LANES - 16, 1), pltpu.roll(x, 16, 1))
    o_ref[:, COL_KROPE:] = jnp.where(lane < QK_ROPE_DIM, x, pltpu.roll(src, QK_ROPE_DIM, 1)).astype(BF16)


def _pack_w_in(w):
    depth, d, d_in = w.shape
    tr = 256
    packed = pl.pallas_call(
        _pack_w_in_kernel,
        out_shape=jax.ShapeDtypeStruct((depth * d, D_IN_PACKED), BF16),
        grid=(depth * d // tr,),
        in_specs=[pl.BlockSpec((tr, d_in), lambda i: (i, 0))],
        out_specs=pl.BlockSpec((tr, D_IN_PACKED), lambda i: (i, 0)),
        compiler_params=_params("arbitrary"),
        name="pack_w_in",
    )(w.reshape(depth * d, d_in))
    return packed.reshape(depth, d, D_IN_PACKED)


def _pack_w_uq(w):
    w = w.reshape(DEPTH, LORA_RANK, MLA_HEADS, QK_HEAD_DIM)
    rope = w[..., QK_NOPE_DIM:]
    return jnp.concatenate([w[..., :QK_NOPE_DIM], rope, rope[..., _ROT_SRC]], axis=-1).reshape(
        DEPTH, LORA_RANK, MLA_HEADS * HEAD_PAD).astype(BF16)


def _pack_w_ukv(w):
    w = w.reshape(DEPTH, LORA_RANK, MLA_HEADS, QK_NOPE_DIM + V_HEAD_DIM)
    return jnp.concatenate([w[..., :QK_NOPE_DIM].reshape(DEPTH, LORA_RANK, -1),
                            w[..., QK_NOPE_DIM:].reshape(DEPTH, LORA_RANK, -1)], axis=-1).astype(BF16)


def _pack_head_w(qw, kw):
    pad = jnp.zeros((LANES - QK_ROPE_DIM,), F32)
    rows = []
    for w in (qw, kw):
        rope = w[QK_NOPE_DIM:]
        rows += [w[:QK_NOPE_DIM], jnp.concatenate([rope, pad]), jnp.concatenate([rope[_ROT_SRC], pad])]
    shift = QK_HEAD_DIM * SM_SCALE * LOG2_E * jnp.max(jnp.abs(qw)) * jnp.max(jnp.abs(kw))
    spare = (np.arange(LANES) == QK_ROPE_DIM).astype(np.float32)
    rows += [-shift * spare, jnp.asarray(spare)]
    return jnp.stack(rows), shift <= MAX_SAFE_SHIFT


def kernel(x, c, ctx, c_ctx, norm_w, w_ada, b_ada, w_in, pool_w, pool_scale, fnet_w, q_norm_w, w_uq,
           kv_norm_w, w_ukv, q_head_norm_w, k_head_norm_w, w_out):
    batch, seq, _ = x.shape
    n_ctx = ctx.shape[1]
    cc = jnp.concatenate([c, c_ctx[None, :], jnp.zeros((SUBLANES - batch - 1, D_MODEL), F32)], axis=0)
    mod_all = _ada_modulation(cc, w_ada, b_ada)

    cs_tab = _channel_dft_table()
    rope_lat = _rope_tables(seq)
    rope_ctx = _identity_rope_tables(n_ctx)
    dft_lat = _position_dft_table(seq)
    dft_ctx = _position_dft_table(n_ctx)
    tm_lat, tm_qkv, tm_ctx = 512, 512, n_ctx

    w_in_p, wuq, wukv = _pack_w_in(w_in), _pack_w_uq(w_uq), _pack_w_ukv(w_ukv)
    pw, fw, wo = _pair_block_diag(pool_w), _pair_block_diag(fnet_w), w_out.astype(BF16)

    for l in range(DEPTH):
        mod_lat = mod_all[l, :batch][:, None, :]
        mod_ctx = mod_all[l, batch:batch + 1][:, None, :]
        nw = norm_w[l][None, :]
        head_w, shift_is_safe = _pack_head_w(q_head_norm_w[l], k_head_norm_w[l])
        qnw, kvnw = q_norm_w[l][None, :], kv_norm_w[l][None, :]
        ps = pool_scale[l][None, :]
        ctx_out = l < DEPTH - 1

        p_c, uv_c = _in_proj(ctx, mod_ctx, nw, w_in_p, cs_tab, layer=l, per_batch_mod=False, tm=tm_ctx)
        q_c, k_c, v_c = _qkv(p_c, qnw, kvnw, wuq, wukv, head_w, *rope_ctx, layer=l, tm=tm_ctx)
        p_l, uv_l = _in_proj(x, mod_lat, nw, w_in_p, cs_tab, layer=l, per_batch_mod=True, tm=tm_lat)
        q_l, k_l, v_l = _qkv(p_l, qnw, kvnw, wuq, wukv, head_w, *rope_lat, layer=l, tm=tm_qkv)

        attn_l = _attention(shift_is_safe, q_l, k_c, v_c, k_l, v_l, tq=2048)
        r_l = _pos_dft(dft_lat, uv_l)
        x_new = _out_proj(x, mod_lat, p_l, r_l, attn_l, pw, ps, fw, wo, layer=l, per_batch_mod=True, tm=tm_lat)
        if ctx_out:
            attn_c = _attention(shift_is_safe, q_c, k_c, v_c, tq=n_ctx)
            r_c = _pos_dft(dft_ctx, uv_c)
            ctx = _out_proj(ctx, mod_ctx, p_c, r_c, attn_c, pw, ps, fw, wo, layer=l, per_batch_mod=False,
                            tm=tm_ctx)
        x = x_new
    return x
```

```python
import functools
import math

import numpy as np
import jax
import jax.numpy as jnp
from jax import lax
from jax.experimental import pallas as pl
from jax.experimental.pallas import tpu as pltpu

D_MODEL = 2048
DEPTH = 2
GRID_W = 64
POOL_WIDTH = 512
POOL_WINDOWS = (2, 4, 8, 16)
GROUP_CH = 128
N_GROUPS = 4
MLA_HEADS = 8
QK_NOPE_DIM = 128
QK_ROPE_DIM = 64
QK_HEAD_DIM = QK_NOPE_DIM + QK_ROPE_DIM
V_HEAD_DIM = 128
MLA_WIDTH = MLA_HEADS * V_HEAD_DIM
LORA_RANK = 512
N_FREQ_PER_AXIS = QK_ROPE_DIM // 4
ROPE_THETA = 10000.0
EPS = 1e-6
SM_SCALE = QK_HEAD_DIM ** -0.5
LOG2_E = math.log2(math.e)
MAX_SAFE_SHIFT = 60.0

COL_POOL = 0
COL_POOL_GATE = 512
COL_FNET = 1024
COL_FNET_GATE = 1536
COL_CQ = 2048
COL_CKV = 2560
COL_MLA_GATE = 3072
COL_KROPE = 4096
D_IN_PACKED = 4224
HEAD_PAD = 256

LANES = 128
SUBLANES = 8
VMEM_LIMIT = 56 * 1024 * 1024
ROW_SUB = 256
VT_ROWS = V_HEAD_DIM + 2 * SUBLANES
ATTN_CHAIN = 1024
QKV_SUB = 256
POOL_HALO = 2 * SUBLANES
BAND_K = ROW_SUB + LANES

F32 = jnp.float32
BF16 = jnp.bfloat16

_ROT_SRC = np.array([i + 16 if (i % 32) < 16 else i - 16 for i in range(QK_ROPE_DIM)])
_ROT_SIGN = np.array([-1.0 if (i % 32) < 16 else 1.0 for i in range(QK_ROPE_DIM)], np.float32)


def _silu(x):
    return x * jax.nn.sigmoid(x)


def _dot(a, b):
    return jnp.dot(a, b, preferred_element_type=F32)


def _params(*sem):
    return pltpu.CompilerParams(dimension_semantics=sem, vmem_limit_bytes=VMEM_LIMIT)


def _resident(shape, layer=None):
    nd = len(shape)
    if layer is None:
        return pl.BlockSpec(shape, lambda *_: (0,) * nd, pipeline_mode=pl.Buffered(1))
    return pl.BlockSpec((None,) + tuple(shape), lambda *_: (layer,) + (0,) * nd, pipeline_mode=pl.Buffered(1))


def _ada_kernel(c_ref, w_ref, b_ref, o_ref):
    s = _silu(c_ref[...]).astype(BF16)
    o_ref[0] = _dot(s, w_ref[0].astype(BF16)) + b_ref[0]


def _ada_modulation(cc, w_ada, b_ada):
    tn = 512
    n3 = 3 * D_MODEL
    return pl.pallas_call(
        _ada_kernel,
        out_shape=jax.ShapeDtypeStruct((DEPTH, SUBLANES, n3), F32),
        grid=(DEPTH, n3 // tn),
        in_specs=[
            pl.BlockSpec((SUBLANES, D_MODEL), lambda l, j: (0, 0)),
            pl.BlockSpec((1, D_MODEL, tn), lambda l, j: (l, 0, j)),
            pl.BlockSpec((1, 1, tn), lambda l, j: (l, 0, j)),
        ],
        out_specs=pl.BlockSpec((1, SUBLANES, tn), lambda l, j: (l, 0, j)),
        compiler_params=_params("arbitrary", "arbitrary"),
        name="ada_modulation",
    )(cc, w_ada, b_ada.reshape(DEPTH, 1, n3))


def _kv_in_proj_kernel(x_ref, mod_ref, nw_ref, w_ref, p_ref):
    mod = mod_ref[0]
    x = x_ref[0]
    r = lax.rsqrt(jnp.mean(x * x, axis=-1, keepdims=True) + EPS)
    h = (x * r * ((1.0 + mod[:, D_MODEL:2 * D_MODEL]) * nw_ref[...]) + mod[:, :D_MODEL]).astype(BF16)
    p_ref[0, :, :LORA_RANK] = _dot(h, w_ref[:, COL_CKV:COL_CKV + LORA_RANK])
    p_ref[0, :, LORA_RANK:] = _dot(h, w_ref[:, COL_KROPE:])


def _in_proj_kernel(x_ref, mod_ref, nw_ref, w_ref, cs_ref, p_ref, uv_ref):
    mod = mod_ref[0]
    shift = mod[:, :D_MODEL]
    scale1 = (1.0 + mod[:, D_MODEL:2 * D_MODEL]) * nw_ref[...]
    tm = x_ref.shape[1]
    for r0 in range(0, tm, ROW_SUB):
        rows = slice(r0, min(r0 + ROW_SUB, tm))
        x = x_ref[0, rows, :]
        r = lax.rsqrt(jnp.mean(x * x, axis=-1, keepdims=True) + EPS)
        h = (x * r * scale1 + shift).astype(BF16)
        chunk = 512
        for c0 in range(0, D_IN_PACKED, chunk):
            c1 = min(c0 + chunk, D_IN_PACKED)
            pc = _dot(h, w_ref[:, c0:c1])
            p_ref[0, rows, c0:c1] = pc
            if c0 == COL_FNET:
                g = pc.astype(BF16)
                for hd in range(N_GROUPS):
                    cols = slice(hd * GROUP_CH, (hd + 1) * GROUP_CH)
                    uv = _dot(g[:, cols], cs_ref[...])
                    uv_ref[0, 0, rows, cols] = uv[:, :GROUP_CH].astype(BF16)
                    uv_ref[0, 1, rows, cols] = uv[:, GROUP_CH:].astype(BF16)


def _kv_in_proj(xin, mod, norm_w, w_packed, *, layer, tm):
    b, n, _ = xin.shape
    width = LORA_RANK + LANES
    return pl.pallas_call(
        _kv_in_proj_kernel,
        out_shape=jax.ShapeDtypeStruct((b, n, width), F32),
        grid=(b, n // tm),
        in_specs=[
            pl.BlockSpec((1, tm, D_MODEL), lambda bi, i: (bi, i, 0)),
            pl.BlockSpec((1, 1, 3 * D_MODEL), lambda bi, i: (0, 0, 0)),
            _resident((1, D_MODEL)),
            _resident((D_MODEL, D_IN_PACKED), layer),
        ],
        out_specs=pl.BlockSpec((1, tm, width), lambda bi, i: (bi, i, 0)),
        compiler_params=_params("arbitrary", "arbitrary"),
        name="kv_in_proj",
    )(xin, mod, norm_w, w_packed)


def _in_proj(xin, mod, norm_w, w_packed, cs_tab, *, layer, per_batch_mod, tm):
    b, n, _ = xin.shape
    mod_idx = (lambda bi, i: (bi, 0, 0)) if per_batch_mod else (lambda bi, i: (0, 0, 0))
    return pl.pallas_call(
        _in_proj_kernel,
        out_shape=(jax.ShapeDtypeStruct((b, n, D_IN_PACKED), F32),
                   jax.ShapeDtypeStruct((b, 2, n, POOL_WIDTH), BF16)),
        grid=(b, n // tm),
        in_specs=[
            pl.BlockSpec((1, tm, D_MODEL), lambda bi, i: (bi, i, 0)),
            pl.BlockSpec((1, 1, 3 * D_MODEL), mod_idx),
            _resident((1, D_MODEL)),
            _resident((D_MODEL, D_IN_PACKED), layer),
            _resident((GROUP_CH, 2 * GROUP_CH)),
        ],
        out_specs=(pl.BlockSpec((1, tm, D_IN_PACKED), lambda bi, i: (bi, i, 0)),
                   pl.BlockSpec((1, 2, tm, POOL_WIDTH), lambda bi, i: (bi, 0, i, 0))),
        compiler_params=_params("arbitrary", "arbitrary"),
        name="in_proj",
    )(xin, mod, norm_w, w_packed, cs_tab)


def _rms(x, w):
    return x * lax.rsqrt(jnp.mean(x * x, axis=-1, keepdims=True) + EPS) * w


def _rope(slab, t_cos, t_sin):
    return slab * t_cos + pltpu.roll(slab, LANES // 2, 1) * t_sin


def _normed_q_head(qh, cos, sin, hw):
    q_nope_w, q_rope_w, q_rot_w, q_pad = hw[0:1], hw[1:2], hw[2:3], hw[6:7]
    qn, qs = qh[:, :LANES], qh[:, LANES:]
    ss = jnp.sum(qn * qn + 0.5 * (qs * qs), axis=-1, keepdims=True)
    r = lax.rsqrt(ss * (1.0 / QK_HEAD_DIM) + EPS) * (SM_SCALE * LOG2_E)
    rope = _rope(qs, cos * q_rope_w, sin * q_rot_w) * r + q_pad
    return jnp.concatenate([qn * r * q_nope_w, rope], axis=-1).astype(BF16)


def _qkv_kernel(*refs, with_q):
    if with_q:
        (cq_ref, ckv_ref, kr_ref, qnw_ref, kvnw_ref, wuq_ref, wukv_ref, hw_ref, cos_ref, sin_ref,
         q_ref, k_ref, v_ref) = refs
    else:
        ckv_ref, kr_ref, kvnw_ref, wukv_ref, hw_ref, cos_ref, sin_ref, k_ref, v_ref = refs
    hw = hw_ref[...]
    k_nope_w, k_rope_w, k_rot_w, k_pad = hw[3:4], hw[4:5], hw[5:6], hw[7:8]
    tm = ckv_ref.shape[1]
    sub = min(QKV_SUB, tm)
    one_row = (lax.broadcasted_iota(jnp.int32, (VT_ROWS - V_HEAD_DIM, sub), 0) == 0).astype(BF16)
    inv_dim = 1.0 / QK_HEAD_DIM

    for r0 in range(0, tm, sub):
        rows = slice(r0, r0 + sub)
        cos = cos_ref[rows, :]
        sin = sin_ref[rows, :]
        if with_q:
            cqn = _rms(cq_ref[0, rows, :], qnw_ref[...]).astype(BF16)
            for hd in range(MLA_HEADS):
                cols = slice(hd * HEAD_PAD, (hd + 1) * HEAD_PAD)
                q_ref[0, rows, cols] = _normed_q_head(_dot(cqn, wuq_ref[:, cols]), cos, sin, hw)

        ckvn = _rms(ckv_ref[0, rows, :], kvnw_ref[...]).astype(BF16)
        ks = kr_ref[0, rows, :]
        ss_rope = 0.5 * jnp.sum(ks * ks, axis=-1, keepdims=True)
        k_rope = _rope(ks, cos * k_rope_w, sin * k_rot_w)
        for hd in range(MLA_HEADS):
            kn = _dot(ckvn, wukv_ref[:, hd * LANES:(hd + 1) * LANES])
            ss = jnp.sum(kn * kn, axis=-1, keepdims=True) + ss_rope
            r = lax.rsqrt(ss * inv_dim + EPS)
            k_ref[0, rows, hd * HEAD_PAD:(hd + 1) * HEAD_PAD] = jnp.concatenate(
                [kn * r * k_nope_w, k_rope * r + k_pad], axis=-1).astype(BF16)
        v = _dot(ckvn, wukv_ref[:, MLA_HEADS * LANES:])
        for hd in range(MLA_HEADS):
            v_ref[0, hd, 0:V_HEAD_DIM, rows] = v[:, hd * LANES:(hd + 1) * LANES].T.astype(BF16)
            v_ref[0, hd, V_HEAD_DIM:, rows] = one_row


def _qkv(p, q_norm_w, kv_norm_w, wuq, wukv, head_w, cos_tab, sin_tab, *, layer, tm, with_q=True):
    b, n, _ = p.shape
    row = lambda bi, i: (i, 0)
    ckv_col, kr_col = (COL_CKV, COL_KROPE) if with_q else (0, LORA_RANK)
    qk_shape = jax.ShapeDtypeStruct((b, n, MLA_HEADS * HEAD_PAD), BF16)
    qk_spec = pl.BlockSpec((1, tm, MLA_HEADS * HEAD_PAD), lambda bi, i: (bi, i, 0))
    ckv_spec = pl.BlockSpec((1, tm, LORA_RANK), lambda bi, i: (bi, i, ckv_col // LORA_RANK))
    kr_spec = pl.BlockSpec((1, tm, LANES), lambda bi, i: (bi, i, kr_col // LANES))
    kv_w_specs = [_resident((1, LORA_RANK)), _resident((LORA_RANK, 2 * MLA_WIDTH), layer)]
    tail_specs = [_resident((SUBLANES, LANES)), pl.BlockSpec((tm, LANES), row), pl.BlockSpec((tm, LANES), row)]
    kv_out = ((qk_shape, jax.ShapeDtypeStruct((b, MLA_HEADS, VT_ROWS, n), BF16)),
              (qk_spec, pl.BlockSpec((1, MLA_HEADS, VT_ROWS, tm), lambda bi, i: (bi, 0, 0, i))))
    if with_q:
        in_specs = [pl.BlockSpec((1, tm, LORA_RANK), lambda bi, i: (bi, i, COL_CQ // LORA_RANK)), ckv_spec, kr_spec,
                    _resident((1, LORA_RANK)), kv_w_specs[0],
                    _resident((LORA_RANK, MLA_HEADS * HEAD_PAD), layer), kv_w_specs[1]] + tail_specs
        args = (p, p, p, q_norm_w, kv_norm_w, wuq, wukv, head_w, cos_tab, sin_tab)
        out_shape, out_specs = (qk_shape,) + kv_out[0], (qk_spec,) + kv_out[1]
    else:
        in_specs = [ckv_spec, kr_spec] + kv_w_specs + tail_specs
        args = (p, p, kv_norm_w, wukv, head_w, cos_tab, sin_tab)
        out_shape, out_specs = kv_out
    return pl.pallas_call(
        functools.partial(_qkv_kernel, with_q=with_q),
        out_shape=out_shape,
        grid=(b, n // tm),
        in_specs=in_specs,
        out_specs=out_specs,
        compiler_params=_params("arbitrary", "arbitrary"),
        name="qkv" if with_q else "kv",
    )(*args)


def _scores_t(k, q):
    return lax.dot_general(k, q, (((1,), (1,)), ((), ())), preferred_element_type=F32)


def _attn_finish(acc, o_ref, rows=slice(None)):
    o_ref[0, rows, :] = (acc[:V_HEAD_DIM] / acc[V_HEAD_DIM:V_HEAD_DIM + 1]).T.astype(BF16)


def _attn_shifted_kernel(*refs, tk, n_lat):
    if n_lat:
        q_ref, kc_ref, vc_ref, kl_ref, vl_ref, o_ref = refs
    else:
        q_ref, kc_ref, vc_ref, o_ref = refs
    tq = q_ref.shape[1]
    cw = min(tq, ATTN_CHAIN)
    chains = [slice(r0, r0 + cw) for r0 in range(0, tq, cw)]
    qs = [q_ref[0, rows, :] for rows in chains]

    def tile(q, k, vt):
        return _dot(vt, jnp.exp2(_scores_t(k, q)).astype(BF16))

    accs = [tile(q, kc_ref[0], vc_ref[0, 0]) for q in qs]
    for j in range(n_lat // tk):
        k, vt = kl_ref[0, j * tk:(j + 1) * tk, :], vl_ref[0, 0, :, j * tk:(j + 1) * tk]
        accs = [acc + tile(q, k, vt) for q, acc in zip(qs, accs)]
    for rows, acc in zip(chains, accs):
        _attn_finish(acc, o_ref, rows)


def _attn_online_kernel(*refs, tk, n_lat):
    if n_lat:
        q_ref, kc_ref, vc_ref, kl_ref, vl_ref, o_ref = refs
    else:
        q_ref, kc_ref, vc_ref, o_ref = refs
    q = q_ref[0]
    tq = q.shape[0]

    def step(k, vt, m, acc):
        s = _scores_t(k, q)
        m_new = jnp.maximum(m, jnp.max(s, axis=0, keepdims=True))
        p = jnp.exp2(s - m_new)
        return m_new, jnp.exp2(m - m_new) * acc + _dot(vt, p.astype(BF16))

    m = jnp.full((1, tq), -jnp.inf, F32)
    acc = jnp.zeros((VT_ROWS, tq), F32)
    m, acc = step(kc_ref[0], vc_ref[0, 0], m, acc)
    if n_lat:
        def body(j, carry):
            j0 = pl.multiple_of(j * tk, tk)
            return step(kl_ref[0, pl.ds(j0, tk), :], vl_ref[0, 0, :, pl.ds(j0, tk)], *carry)
        m, acc = lax.fori_loop(0, n_lat // tk, body, (m, acc))
    _attn_finish(acc, o_ref)


def _attention(shift_is_safe, q, k_ctx, v_ctx, k_lat=None, v_lat=None, *, tq, tk=4096):
    b, nq, _ = q.shape
    n_ctx = k_ctx.shape[1]
    n_lat = 0 if k_lat is None else k_lat.shape[1]
    head = lambda bi, hd, i: (bi, 0, hd)
    head_t = lambda bi, hd, i: (bi, hd, 0, 0)
    in_specs = [
        pl.BlockSpec((1, tq, HEAD_PAD), lambda bi, hd, i: (bi, i, hd)),
        pl.BlockSpec((1, n_ctx, HEAD_PAD), head),
        pl.BlockSpec((1, 1, VT_ROWS, n_ctx), head_t),
    ]
    args = [q, k_ctx, v_ctx]
    if n_lat:
        in_specs += [pl.BlockSpec((1, n_lat, HEAD_PAD), head), pl.BlockSpec((1, 1, VT_ROWS, n_lat), head_t)]
        args += [k_lat, v_lat]

    def run(body, name):
        return pl.pallas_call(
            functools.partial(body, tk=tk, n_lat=n_lat),
            out_shape=jax.ShapeDtypeStruct((b, nq, MLA_WIDTH), BF16),
            grid=(b, MLA_HEADS, nq // tq),
            in_specs=in_specs,
            out_specs=pl.BlockSpec((1, tq, V_HEAD_DIM), lambda bi, hd, i: (bi, i, hd)),
            compiler_params=_params("arbitrary", "arbitrary", "arbitrary"),
            name=name,
        )

    return lax.cond(shift_is_safe,
                    lambda *a: run(_attn_shifted_kernel, "attention_shifted")(*a),
                    lambda *a: run(_attn_online_kernel, "attention_online")(*a),
                    *args)


def _pos_dft_kernel(tab_ref, uv_ref, o_ref, ue_ref, vo_ref, dp_ref, *, n):
    half = n // 2
    tr = min(ROW_SUB, half)
    tk = min(2 * ROW_SUB, half)
    inv = 1.0 / math.sqrt(n)
    u_mid = uv_ref[0, 0, half:half + 1, :].astype(F32) * inv
    anti = (lax.broadcasted_iota(jnp.int32, (tr, tr), 0) + lax.broadcasted_iota(jnp.int32, (tr, tr), 1)
            == tr).astype(BF16)
    first = lax.broadcasted_iota(jnp.int32, (tr, 1), 0) == 0

    def reverse_after(tile, head_row):
        return jnp.where(first, head_row.astype(F32), _dot(anti, tile))

    for t in range(half // tr):
        lo = slice(t * tr, (t + 1) * tr)
        hi = slice(n - (t + 1) * tr, n - t * tr)
        for c, (dst, sign) in enumerate(((ue_ref, 1.0), (vo_ref, -1.0))):
            head = uv_ref[0, c, n - t * tr:n - t * tr + 1, :] if t else jnp.zeros((1, POOL_WIDTH), BF16)
            partner = reverse_after(uv_ref[0, c, hi, :], head)
            dst[lo, :] = (uv_ref[0, c, lo, :].astype(F32) + sign * partner).astype(BF16)

    for i in range(half // tk):
        rows = slice(i * tk, (i + 1) * tk)
        k = i * tk + lax.broadcasted_iota(jnp.int32, (tk, 1), 0)
        p = _dot(tab_ref[rows, :half], ue_ref[...]) + (1 - 2 * (k % 2)).astype(F32) * u_mid
        q = _dot(tab_ref[rows, half:], vo_ref[...])
        o_ref[0, rows, :] = (p - q).astype(BF16)
        dp_ref[rows, :] = (p + q).astype(BF16)

    j = lax.broadcasted_iota(jnp.int32, (SUBLANES, half), 1)
    alt = ((1 - 2 * (j % 2)).astype(F32) * inv).astype(BF16)
    mid = (_dot(alt, ue_ref[...])[0:1] + u_mid).astype(BF16)

    for s in range(half // tr):
        head = dp_ref[half - s * tr:half - s * tr + 1, :] if s else mid
        tile = dp_ref[half - (s + 1) * tr:half - s * tr, :]
        o_ref[0, half + s * tr:half + (s + 1) * tr, :] = reverse_after(tile, head).astype(BF16)


def _pos_dft(tab, uv):
    b, _, n, _ = uv.shape
    half = n // 2
    return pl.pallas_call(
        functools.partial(_pos_dft_kernel, n=n),
        out_shape=jax.ShapeDtypeStruct((b, n, POOL_WIDTH), BF16),
        grid=(b,),
        in_specs=[_resident((half, n)),
                  pl.BlockSpec((1, 2, n, POOL_WIDTH), lambda bi: (bi, 0, 0, 0))],
        out_specs=pl.BlockSpec((1, n, POOL_WIDTH), lambda bi: (bi, 0, 0)),
        scratch_shapes=[pltpu.VMEM((half, POOL_WIDTH), BF16)] * 3,
        compiler_params=_params("arbitrary"),
        name="pos_dft",
    )(tab, uv)


def _out_proj_kernel(x_ref, mod_ref, pu_ref, prev_ref, next_ref, pg_ref, fg_ref, mg_ref, r_ref, at_ref,
                     band_ref, pw_ref, ps_ref, fw_ref, wo_ref, o_ref, ext_ref, *, tm, n):
    i = pl.program_id(1)
    ext_ref[0:POOL_HALO] = jnp.where(i > 0, prev_ref[0], 0.0).astype(BF16)
    ext_ref[POOL_HALO:POOL_HALO + tm] = pu_ref[0].astype(BF16)
    ext_ref[POOL_HALO + tm:2 * POOL_HALO + tm] = jnp.where(i < pl.num_programs(1) - 1, next_ref[0], 0.0).astype(BF16)
    ext_ref[2 * POOL_HALO + tm:] = jnp.zeros((BAND_K - ROW_SUB - 2 * POOL_HALO, POOL_WIDTH), BF16)
    gate = mod_ref[0][:, 2 * D_MODEL:]
    pair = 2 * GROUP_CH
    for r0 in range(0, tm, ROW_SUB):
        rows = slice(r0, r0 + ROW_SUB)
        t = i * tm + r0 + lax.broadcasted_iota(jnp.int32, (ROW_SUB, 1), 0)
        pooled = []
        for g, w in enumerate(POOL_WINDOWS):
            cols = slice(g * GROUP_CH, (g + 1) * GROUP_CH)
            lo = w // 2
            hi = w - lo - 1
            s = _dot(band_ref[g], ext_ref[r0:r0 + BAND_K, cols])
            cnt = (jnp.minimum(t + hi, n - 1) - jnp.maximum(t - lo, 0) + 1).astype(F32)
            pooled.append((s / cnt - pu_ref[0, rows, cols]).astype(BF16))
        parts = []
        for j in range(N_GROUPS // 2):
            cols = slice(j * pair, (j + 1) * pair)
            y = _dot(jnp.concatenate(pooled[2 * j:2 * j + 2], axis=-1), pw_ref[j]) * ps_ref[:, cols]
            parts.append((_silu(pg_ref[0, rows, cols]) * y).astype(BF16))
        for j in range(N_GROUPS // 2):
            cols = slice(j * pair, (j + 1) * pair)
            f = _dot(r_ref[0, rows, cols], fw_ref[j])
            parts.append((_silu(fg_ref[0, rows, cols]) * f).astype(BF16))
        parts.append((_silu(mg_ref[0, rows, :]) * at_ref[0, rows, :].astype(F32)).astype(BF16))
        mixed = jnp.concatenate(parts, axis=-1)
        chunk = 512
        for c0 in range(0, D_MODEL, chunk):
            y = _dot(mixed, wo_ref[:, c0:c0 + chunk])
            o_ref[0, rows, c0:c0 + chunk] = x_ref[0, rows, c0:c0 + chunk] + gate[:, c0:c0 + chunk] * y


def _out_proj(xin, mod, p, r, attn, pool_w, pool_scale, fnet_w, w_out, *, layer, per_batch_mod, tm):
    b, n, _ = xin.shape
    mod_idx = (lambda bi, i: (bi, 0, 0)) if per_batch_mod else (lambda bi, i: (0, 0, 0))
    hb = tm // POOL_HALO
    last = n // POOL_HALO - 1
    pcol = lambda width, off: pl.BlockSpec((1, tm, width), lambda bi, i: (bi, i, off // width))
    pair = 2 * GROUP_CH
    return pl.pallas_call(
        functools.partial(_out_proj_kernel, tm=tm, n=n),
        out_shape=jax.ShapeDtypeStruct((b, n, D_MODEL), F32),
        grid=(b, n // tm),
        in_specs=[
            pl.BlockSpec((1, tm, D_MODEL), lambda bi, i: (bi, i, 0)),
            pl.BlockSpec((1, 1, 3 * D_MODEL), mod_idx),
            pcol(POOL_WIDTH, COL_POOL),
            pl.BlockSpec((1, POOL_HALO, POOL_WIDTH), lambda bi, i: (bi, jnp.maximum(i * hb - 1, 0), 0)),
            pl.BlockSpec((1, POOL_HALO, POOL_WIDTH), lambda bi, i: (bi, jnp.minimum((i + 1) * hb, last), 0)),
            pcol(POOL_WIDTH, COL_POOL_GATE),
            pcol(POOL_WIDTH, COL_FNET_GATE),
            pcol(MLA_WIDTH, COL_MLA_GATE),
            pl.BlockSpec((1, tm, POOL_WIDTH), lambda bi, i: (bi, i, 0)),
            pl.BlockSpec((1, tm, MLA_WIDTH), lambda bi, i: (bi, i, 0)),
            _resident((N_GROUPS, ROW_SUB, BAND_K)),
            _resident((N_GROUPS // 2, pair, pair), layer),
            _resident((1, POOL_WIDTH)),
            _resident((N_GROUPS // 2, pair, pair), layer),
            _resident((D_MODEL, D_MODEL), layer),
        ],
        out_specs=pl.BlockSpec((1, tm, D_MODEL), lambda bi, i: (bi, i, 0)),
        scratch_shapes=[pltpu.VMEM((tm + BAND_K - ROW_SUB, POOL_WIDTH), BF16)],
        compiler_params=_params("arbitrary", "arbitrary"),
        name="out_proj",
    )(xin, mod, p, p, p, p, p, p, r, attn, _pool_band_table(), pool_w, pool_scale, fnet_w, w_out)


def _rope_tables(n):
    t = jnp.arange(n, dtype=jnp.int32)
    inv_freq = ROPE_THETA ** (-jnp.arange(N_FREQ_PER_AXIS, dtype=F32) / N_FREQ_PER_AXIS)
    ang_r = (t // GRID_W).astype(F32)[:, None] * inv_freq
    ang_c = (t % GRID_W).astype(F32)[:, None] * inv_freq
    ang = jnp.concatenate([ang_r, ang_r, ang_c, ang_c], axis=-1)
    pad = jnp.zeros((n, LANES - QK_ROPE_DIM), F32)
    return (jnp.concatenate([jnp.cos(ang), pad], axis=-1),
            jnp.concatenate([jnp.sin(ang) * _ROT_SIGN, pad], axis=-1))


def _identity_rope_tables(n):
    pad = jnp.zeros((n, LANES - QK_ROPE_DIM), F32)
    return jnp.concatenate([jnp.ones((n, QK_ROPE_DIM), F32), pad], axis=-1), jnp.zeros((n, LANES), F32)


def _channel_dft_table():
    k = np.arange(GROUP_CH)
    ang = 2.0 * np.pi * ((k[:, None] * k[None, :]) % GROUP_CH) / GROUP_CH
    tab = np.concatenate([np.cos(ang), np.sin(ang)], axis=1) / math.sqrt(GROUP_CH)
    return jnp.asarray(tab, F32).astype(BF16)


def _position_dft_table(n):
    half = n // 2
    g = 1
    while g * g < n:
        g *= 2
    period = n // g
    j = jnp.arange(half, dtype=jnp.int32)
    a = jnp.arange(half // g, dtype=jnp.int32)
    bb = jnp.arange(g, dtype=jnp.int32)
    ang_a = ((a[:, None] * j[None, :]) % period).astype(F32) * (2.0 * math.pi / period)
    ang_b = ((bb[:, None] * j[None, :]) % n).astype(F32) * (2.0 * math.pi / n)
    ca, sa = jnp.cos(ang_a)[:, None, :], jnp.sin(ang_a)[:, None, :]
    cb, sb = jnp.cos(ang_b)[None, :, :], jnp.sin(ang_b)[None, :, :]
    scale = 1.0 / math.sqrt(n)
    cos = ((ca * cb - sa * sb) * scale).reshape(half, half)
    sin = ((sa * cb + ca * sb) * scale).reshape(half, half)
    return jnp.concatenate([cos, sin], axis=1).astype(BF16)


def _pool_band_table():
    t = np.arange(ROW_SUB)[:, None]
    d = np.arange(BAND_K)[None, :] - POOL_HALO - t
    bands = [((d >= -(w // 2)) & (d <= w - w // 2 - 1)).astype(np.float32) for w in POOL_WINDOWS]
    return jnp.asarray(np.stack(bands), BF16)


def _pair_block_diag(w):
    even, odd = w[:, 0::2], w[:, 1::2]
    zero = jnp.zeros_like(even)
    return jnp.concatenate([jnp.concatenate([even, zero], axis=-1),
                            jnp.concatenate([zero, odd], axis=-1)], axis=-2).astype(BF16)


def _pack_w_in_kernel(w_ref, o_ref):
    w = w_ref[...]
    rows = w.shape[0]
    o_ref[:, :COL_MLA_GATE] = w[:, :COL_MLA_GATE].astype(BF16)
    o_ref[:, COL_MLA_GATE:COL_KROPE] = w[:, COL_MLA_GATE + QK_ROPE_DIM:].astype(BF16)
    x = w[:, COL_MLA_GATE:COL_MLA_GATE + LANES]
    lane = lax.broadcasted_iota(jnp.int32, (rows, LANES), 1)
    src = jnp.where(lane % 32 < 16, pltpu.roll(x, LANES - 16, 1), pltpu.roll(x, 16, 1))
    o_ref[:, COL_KROPE:] = jnp.where(lane < QK_ROPE_DIM, x, pltpu.roll(src, QK_ROPE_DIM, 1)).astype(BF16)


def _pack_w_in(w):
    depth, d, d_in = w.shape
    tr = 256
    packed = pl.pallas_call(
        _pack_w_in_kernel,
        out_shape=jax.ShapeDtypeStruct((depth * d, D_IN_PACKED), BF16),
        grid=(depth * d // tr,),
        in_specs=[pl.BlockSpec((tr, d_in), lambda i: (i, 0))],
        out_specs=pl.BlockSpec((tr, D_IN_PACKED), lambda i: (i, 0)),
        compiler_params=_params("arbitrary"),
        name="pack_w_in",
    )(w.reshape(depth * d, d_in))
    return packed.reshape(depth, d, D_IN_PACKED)


def _pack_w_uq(w):
    w = w.reshape(DEPTH, LORA_RANK, MLA_HEADS, QK_HEAD_DIM)
    rope = w[..., QK_NOPE_DIM:]
    return jnp.concatenate([w[..., :QK_NOPE_DIM], rope, rope[..., _ROT_SRC]], axis=-1).reshape(
        DEPTH, LORA_RANK, MLA_HEADS * HEAD_PAD).astype(BF16)


def _pack_w_ukv(w):
    w = w.reshape(DEPTH, LORA_RANK, MLA_HEADS, QK_NOPE_DIM + V_HEAD_DIM)
    return jnp.concatenate([w[..., :QK_NOPE_DIM].reshape(DEPTH, LORA_RANK, -1),
                            w[..., QK_NOPE_DIM:].reshape(DEPTH, LORA_RANK, -1)], axis=-1).astype(BF16)


def _pack_head_w(qw, kw):
    pad = jnp.zeros((LANES - QK_ROPE_DIM,), F32)
    rows = []
    for w in (qw, kw):
        rope = w[QK_NOPE_DIM:]
        rows += [w[:QK_NOPE_DIM], jnp.concatenate([rope, pad]), jnp.concatenate([rope[_ROT_SRC], pad])]
    shift = QK_HEAD_DIM * SM_SCALE * LOG2_E * jnp.max(jnp.abs(qw)) * jnp.max(jnp.abs(kw))
    spare = (np.arange(LANES) == QK_ROPE_DIM).astype(np.float32)
    rows += [-shift * spare, jnp.asarray(spare)]
    return jnp.stack(rows), shift <= MAX_SAFE_SHIFT


def kernel(x, c, ctx, c_ctx, norm_w, w_ada, b_ada, w_in, pool_w, pool_scale, fnet_w, q_norm_w, w_uq,
           kv_norm_w, w_ukv, q_head_norm_w, k_head_norm_w, w_out):
    batch, seq, _ = x.shape
    n_ctx = ctx.shape[1]
    cc = jnp.concatenate([c, c_ctx[None, :], jnp.zeros((SUBLANES - batch - 1, D_MODEL), F32)], axis=0)
    mod_all = _ada_modulation(cc, w_ada, b_ada)

    cs_tab = _channel_dft_table()
    rope_lat = _rope_tables(seq)
    rope_ctx = _identity_rope_tables(n_ctx)
    dft_lat = _position_dft_table(seq)
    dft_ctx = _position_dft_table(n_ctx)
    tm_lat, tm_qkv, tm_ctx = 512, 512, n_ctx

    w_in_p, wuq, wukv = _pack_w_in(w_in), _pack_w_uq(w_uq), _pack_w_ukv(w_ukv)
    pw, fw, wo = _pair_block_diag(pool_w), _pair_block_diag(fnet_w), w_out.astype(BF16)

    for l in range(DEPTH):
        mod_lat = mod_all[l, :batch][:, None, :]
        mod_ctx = mod_all[l, batch:batch + 1][:, None, :]
        nw = norm_w[l][None, :]
        head_w, shift_is_safe = _pack_head_w(q_head_norm_w[l], k_head_norm_w[l])
        qnw, kvnw = q_norm_w[l][None, :], kv_norm_w[l][None, :]
        ps = pool_scale[l][None, :]
        ctx_out = l < DEPTH - 1

        if ctx_out:
            p_c, uv_c = _in_proj(ctx, mod_ctx, nw, w_in_p, cs_tab, layer=l, per_batch_mod=False, tm=tm_ctx)
            q_c, k_c, v_c = _qkv(p_c, qnw, kvnw, wuq, wukv, head_w, *rope_ctx, layer=l, tm=tm_ctx)
        else:
            p_c = _kv_in_proj(ctx, mod_ctx, nw, w_in_p, layer=l, tm=tm_ctx)
            k_c, v_c = _qkv(p_c, qnw, kvnw, wuq, wukv, head_w, *rope_ctx, layer=l, tm=tm_ctx, with_q=False)
        p_l, uv_l = _in_proj(x, mod_lat, nw, w_in_p, cs_tab, layer=l, per_batch_mod=True, tm=tm_lat)
        q_l, k_l, v_l = _qkv(p_l, qnw, kvnw, wuq, wukv, head_w, *rope_lat, layer=l, tm=tm_qkv)

        attn_l = _attention(shift_is_safe, q_l, k_c, v_c, k_l, v_l, tq=2048)
        r_l = _pos_dft(dft_lat, uv_l)
        x_new = _out_proj(x, mod_lat, p_l, r_l, attn_l, pw, ps, fw, wo, layer=l, per_batch_mod=True, tm=tm_lat)
        if ctx_out:
            attn_c = _attention(shift_is_safe, q_c, k_c, v_c, tq=n_ctx)
            r_c = _pos_dft(dft_ctx, uv_c)
            ctx = _out_proj(ctx, mod_ctx, p_c, r_c, attn_c, pw, ps, fw, wo, layer=l, per_batch_mod=False,
                            tm=tm_ctx)
        x = x_new
    return x
```

```python
import functools
import math

import numpy as np
import jax
import jax.numpy as jnp
from jax import lax
from jax.experimental import pallas as pl
from jax.experimental.pallas import tpu as pltpu

D_MODEL = 2048
DEPTH = 2
GRID_W = 64
POOL_WIDTH = 512
POOL_WINDOWS = (2, 4, 8, 16)
GROUP_CH = 128
N_GROUPS = 4
MLA_HEADS = 8
QK_NOPE_DIM = 128
QK_ROPE_DIM = 64
QK_HEAD_DIM = QK_NOPE_DIM + QK_ROPE_DIM
V_HEAD_DIM = 128
MLA_WIDTH = MLA_HEADS * V_HEAD_DIM
LORA_RANK = 512
N_FREQ_PER_AXIS = QK_ROPE_DIM // 4
ROPE_THETA = 10000.0
EPS = 1e-6
SM_SCALE = QK_HEAD_DIM ** -0.5
LOG2_E = math.log2(math.e)
MAX_SAFE_SHIFT = 60.0

COL_POOL = 0
COL_POOL_GATE = 512
COL_FNET = 1024
COL_FNET_GATE = 1536
COL_CQ = 2048
COL_CKV = 2560
COL_MLA_GATE = 3072
COL_KROPE = 4096
D_IN_PACKED = 4224
HEAD_PAD = 256

LANES = 128
SUBLANES = 8
VMEM_LIMIT = 56 * 1024 * 1024
ROW_SUB = 256
VT_ROWS = V_HEAD_DIM + 2 * SUBLANES
ATTN_CHAIN = 1024
QKV_SUB = 256
POOL_HALO = 2 * SUBLANES
BAND_K = ROW_SUB + LANES

F32 = jnp.float32
BF16 = jnp.bfloat16

_ROT_SRC = np.array([i + 16 if (i % 32) < 16 else i - 16 for i in range(QK_ROPE_DIM)])
_ROT_SIGN = np.array([-1.0 if (i % 32) < 16 else 1.0 for i in range(QK_ROPE_DIM)], np.float32)


def _silu(x):
    return x * jax.nn.sigmoid(x)


def _dot(a, b):
    return jnp.dot(a, b, preferred_element_type=F32)


def _params(*sem):
    return pltpu.CompilerParams(dimension_semantics=sem, vmem_limit_bytes=VMEM_LIMIT)


def _resident(shape, layer=None):
    nd = len(shape)
    if layer is None:
        return pl.BlockSpec(shape, lambda *_: (0,) * nd, pipeline_mode=pl.Buffered(1))
    return pl.BlockSpec((None,) + tuple(shape), lambda *_: (layer,) + (0,) * nd, pipeline_mode=pl.Buffered(1))


def _ada_kernel(c_ref, w_ref, b_ref, o_ref):
    s = _silu(c_ref[...]).astype(BF16)
    o_ref[0] = _dot(s, w_ref[0].astype(BF16)) + b_ref[0]


def _ada_modulation(cc, w_ada, b_ada):
    tn = 512
    n3 = 3 * D_MODEL
    return pl.pallas_call(
        _ada_kernel,
        out_shape=jax.ShapeDtypeStruct((DEPTH, SUBLANES, n3), F32),
        grid=(DEPTH, n3 // tn),
        in_specs=[
            pl.BlockSpec((SUBLANES, D_MODEL), lambda l, j: (0, 0)),
            pl.BlockSpec((1, D_MODEL, tn), lambda l, j: (l, 0, j)),
            pl.BlockSpec((1, 1, tn), lambda l, j: (l, 0, j)),
        ],
        out_specs=pl.BlockSpec((1, SUBLANES, tn), lambda l, j: (l, 0, j)),
        compiler_params=_params("arbitrary", "arbitrary"),
        name="ada_modulation",
    )(cc, w_ada, b_ada.reshape(DEPTH, 1, n3))


def _kv_in_proj_kernel(x_ref, mod_ref, nw_ref, w_ref, p_ref):
    mod = mod_ref[0]
    x = x_ref[0]
    r = lax.rsqrt(jnp.mean(x * x, axis=-1, keepdims=True) + EPS)
    h = (x * r * ((1.0 + mod[:, D_MODEL:2 * D_MODEL]) * nw_ref[...]) + mod[:, :D_MODEL]).astype(BF16)
    p_ref[0, :, :LORA_RANK] = _dot(h, w_ref[:, COL_CKV:COL_CKV + LORA_RANK])
    p_ref[0, :, LORA_RANK:] = _dot(h, w_ref[:, COL_KROPE:])


def _in_proj_kernel(x_ref, mod_ref, nw_ref, w_ref, cs_ref, p_ref, uv_ref):
    mod = mod_ref[0]
    shift = mod[:, :D_MODEL]
    scale1 = (1.0 + mod[:, D_MODEL:2 * D_MODEL]) * nw_ref[...]
    tm = x_ref.shape[1]
    for r0 in range(0, tm, ROW_SUB):
        rows = slice(r0, min(r0 + ROW_SUB, tm))
        x = x_ref[0, rows, :]
        r = lax.rsqrt(jnp.mean(x * x, axis=-1, keepdims=True) + EPS)
        h = (x * r * scale1 + shift).astype(BF16)
        chunk = 512
        for c0 in range(0, D_IN_PACKED, chunk):
            c1 = min(c0 + chunk, D_IN_PACKED)
            pc = _dot(h, w_ref[:, c0:c1])
            p_ref[0, rows, c0:c1] = pc
            if c0 == COL_FNET:
                g = pc.astype(BF16)
                for hd in range(N_GROUPS):
                    cols = slice(hd * GROUP_CH, (hd + 1) * GROUP_CH)
                    uv = _dot(g[:, cols], cs_ref[...])
                    uv_ref[0, 0, rows, cols] = uv[:, :GROUP_CH].astype(BF16)
                    uv_ref[0, 1, rows, cols] = uv[:, GROUP_CH:].astype(BF16)


def _kv_in_proj(xin, mod, norm_w, w_packed, *, layer, tm):
    b, n, _ = xin.shape
    width = LORA_RANK + LANES
    return pl.pallas_call(
        _kv_in_proj_kernel,
        out_shape=jax.ShapeDtypeStruct((b, n, width), F32),
        grid=(b, n // tm),
        in_specs=[
            pl.BlockSpec((1, tm, D_MODEL), lambda bi, i: (bi, i, 0)),
            pl.BlockSpec((1, 1, 3 * D_MODEL), lambda bi, i: (0, 0, 0)),
            _resident((1, D_MODEL)),
            _resident((D_MODEL, D_IN_PACKED), layer),
        ],
        out_specs=pl.BlockSpec((1, tm, width), lambda bi, i: (bi, i, 0)),
        compiler_params=_params("arbitrary", "arbitrary"),
        name="kv_in_proj",
    )(xin, mod, norm_w, w_packed)


def _in_proj(xin, mod, norm_w, w_packed, cs_tab, *, layer, per_batch_mod, tm):
    b, n, _ = xin.shape
    mod_idx = (lambda bi, i: (bi, 0, 0)) if per_batch_mod else (lambda bi, i: (0, 0, 0))
    return pl.pallas_call(
        _in_proj_kernel,
        out_shape=(jax.ShapeDtypeStruct((b, n, D_IN_PACKED), F32),
                   jax.ShapeDtypeStruct((b, 2, n, POOL_WIDTH), BF16)),
        grid=(b, n // tm),
        in_specs=[
            pl.BlockSpec((1, tm, D_MODEL), lambda bi, i: (bi, i, 0)),
            pl.BlockSpec((1, 1, 3 * D_MODEL), mod_idx),
            _resident((1, D_MODEL)),
            _resident((D_MODEL, D_IN_PACKED), layer),
            _resident((GROUP_CH, 2 * GROUP_CH)),
        ],
        out_specs=(pl.BlockSpec((1, tm, D_IN_PACKED), lambda bi, i: (bi, i, 0)),
                   pl.BlockSpec((1, 2, tm, POOL_WIDTH), lambda bi, i: (bi, 0, i, 0))),
        compiler_params=_params("arbitrary", "arbitrary"),
        name="in_proj",
    )(xin, mod, norm_w, w_packed, cs_tab)


def _rms(x, w):
    return x * lax.rsqrt(jnp.mean(x * x, axis=-1, keepdims=True) + EPS) * w


def _rope(slab, t_cos, t_sin):
    return slab * t_cos + pltpu.roll(slab, LANES // 2, 1) * t_sin


def _normed_q_head(qh, cos, sin, hw):
    q_nope_w, q_rope_w, q_rot_w, q_pad = hw[0:1], hw[1:2], hw[2:3], hw[6:7]
    qn, qs = qh[:, :LANES], qh[:, LANES:]
    ss = jnp.sum(qn * qn + 0.5 * (qs * qs), axis=-1, keepdims=True)
    r = lax.rsqrt(ss * (1.0 / QK_HEAD_DIM) + EPS) * (SM_SCALE * LOG2_E)
    rope = _rope(qs, cos * q_rope_w, sin * q_rot_w) * r + q_pad
    return jnp.concatenate([qn * r * q_nope_w, rope], axis=-1).astype(BF16)


def _qkv_kernel(*refs, with_q):
    if with_q:
        (cq_ref, ckv_ref, kr_ref, qnw_ref, kvnw_ref, wuq_ref, wukv_ref, hw_ref, cos_ref, sin_ref,
         q_ref, k_ref, v_ref) = refs
    else:
        ckv_ref, kr_ref, kvnw_ref, wukv_ref, hw_ref, cos_ref, sin_ref, k_ref, v_ref = refs
    hw = hw_ref[...]
    k_nope_w, k_rope_w, k_rot_w, k_pad = hw[3:4], hw[4:5], hw[5:6], hw[7:8]
    tm = ckv_ref.shape[1]
    sub = min(QKV_SUB, tm)
    one_row = (lax.broadcasted_iota(jnp.int32, (VT_ROWS - V_HEAD_DIM, sub), 0) == 0).astype(BF16)
    inv_dim = 1.0 / QK_HEAD_DIM

    for r0 in range(0, tm, sub):
        rows = slice(r0, r0 + sub)
        cos = cos_ref[rows, :]
        sin = sin_ref[rows, :]
        if with_q:
            cqn = _rms(cq_ref[0, rows, :], qnw_ref[...]).astype(BF16)
            for hd in range(MLA_HEADS):
                cols = slice(hd * HEAD_PAD, (hd + 1) * HEAD_PAD)
                q_ref[0, rows, cols] = _normed_q_head(_dot(cqn, wuq_ref[:, cols]), cos, sin, hw)

        ckvn = _rms(ckv_ref[0, rows, :], kvnw_ref[...]).astype(BF16)
        ks = kr_ref[0, rows, :]
        ss_rope = 0.5 * jnp.sum(ks * ks, axis=-1, keepdims=True)
        k_rope = _rope(ks, cos * k_rope_w, sin * k_rot_w)
        for hd in range(MLA_HEADS):
            kn = _dot(ckvn, wukv_ref[:, hd * LANES:(hd + 1) * LANES])
            ss = jnp.sum(kn * kn, axis=-1, keepdims=True) + ss_rope
            r = lax.rsqrt(ss * inv_dim + EPS)
            k_ref[0, rows, hd * HEAD_PAD:(hd + 1) * HEAD_PAD] = jnp.concatenate(
                [kn * r * k_nope_w, k_rope * r + k_pad], axis=-1).astype(BF16)
        v = _dot(ckvn, wukv_ref[:, MLA_HEADS * LANES:])
        for hd in range(MLA_HEADS):
            v_ref[0, hd, 0:V_HEAD_DIM, rows] = v[:, hd * LANES:(hd + 1) * LANES].T.astype(BF16)
            v_ref[0, hd, V_HEAD_DIM:, rows] = one_row


def _qkv(p, q_norm_w, kv_norm_w, wuq, wukv, head_w, cos_tab, sin_tab, *, layer, tm, with_q=True):
    b, n, _ = p.shape
    row = lambda bi, i: (i, 0)
    ckv_col, kr_col = (COL_CKV, COL_KROPE) if with_q else (0, LORA_RANK)
    qk_shape = jax.ShapeDtypeStruct((b, n, MLA_HEADS * HEAD_PAD), BF16)
    qk_spec = pl.BlockSpec((1, tm, MLA_HEADS * HEAD_PAD), lambda bi, i: (bi, i, 0))
    ckv_spec = pl.BlockSpec((1, tm, LORA_RANK), lambda bi, i: (bi, i, ckv_col // LORA_RANK))
    kr_spec = pl.BlockSpec((1, tm, LANES), lambda bi, i: (bi, i, kr_col // LANES))
    kv_w_specs = [_resident((1, LORA_RANK)), _resident((LORA_RANK, 2 * MLA_WIDTH), layer)]
    tail_specs = [_resident((SUBLANES, LANES)), pl.BlockSpec((tm, LANES), row), pl.BlockSpec((tm, LANES), row)]
    kv_out = ((qk_shape, jax.ShapeDtypeStruct((b, MLA_HEADS, VT_ROWS, n), BF16)),
              (qk_spec, pl.BlockSpec((1, MLA_HEADS, VT_ROWS, tm), lambda bi, i: (bi, 0, 0, i))))
    if with_q:
        in_specs = [pl.BlockSpec((1, tm, LORA_RANK), lambda bi, i: (bi, i, COL_CQ // LORA_RANK)), ckv_spec, kr_spec,
                    _resident((1, LORA_RANK)), kv_w_specs[0],
                    _resident((LORA_RANK, MLA_HEADS * HEAD_PAD), layer), kv_w_specs[1]] + tail_specs
        args = (p, p, p, q_norm_w, kv_norm_w, wuq, wukv, head_w, cos_tab, sin_tab)
        out_shape, out_specs = (qk_shape,) + kv_out[0], (qk_spec,) + kv_out[1]
    else:
        in_specs = [ckv_spec, kr_spec] + kv_w_specs + tail_specs
        args = (p, p, kv_norm_w, wukv, head_w, cos_tab, sin_tab)
        out_shape, out_specs = kv_out
    return pl.pallas_call(
        functools.partial(_qkv_kernel, with_q=with_q),
        out_shape=out_shape,
        grid=(b, n // tm),
        in_specs=in_specs,
        out_specs=out_specs,
        compiler_params=_params("arbitrary", "arbitrary"),
        name="qkv" if with_q else "kv",
    )(*args)


def _scores_t(k, q):
    return lax.dot_general(k, q, (((1,), (1,)), ((), ())), preferred_element_type=F32)


def _attn_finish(acc, o_ref, rows=slice(None)):
    o_ref[0, rows, :] = (acc[:V_HEAD_DIM] / acc[V_HEAD_DIM:V_HEAD_DIM + 1]).T.astype(BF16)


def _attn_shifted_kernel(*refs, tk, n_lat):
    if n_lat:
        q_ref, kc_ref, vc_ref, kl_ref, vl_ref, o_ref = refs
    else:
        q_ref, kc_ref, vc_ref, o_ref = refs
    tq = q_ref.shape[1]
    cw = min(tq, ATTN_CHAIN)
    chains = [slice(r0, r0 + cw) for r0 in range(0, tq, cw)]
    qs = [q_ref[0, rows, :] for rows in chains]

    def tile(q, k, vt):
        return _dot(vt, jnp.exp2(_scores_t(k, q)).astype(BF16))

    accs = [tile(q, kc_ref[0], vc_ref[0, 0]) for q in qs]
    for j in range(n_lat // tk):
        k, vt = kl_ref[0, j * tk:(j + 1) * tk, :], vl_ref[0, 0, :, j * tk:(j + 1) * tk]
        accs = [acc + tile(q, k, vt) for q, acc in zip(qs, accs)]
    for rows, acc in zip(chains, accs):
        _attn_finish(acc, o_ref, rows)


def _attn_online_kernel(*refs, tk, n_lat):
    if n_lat:
        q_ref, kc_ref, vc_ref, kl_ref, vl_ref, o_ref = refs
    else:
        q_ref, kc_ref, vc_ref, o_ref = refs
    q = q_ref[0]
    tq = q.shape[0]

    def step(k, vt, m, acc):
        s = _scores_t(k, q)
        m_new = jnp.maximum(m, jnp.max(s, axis=0, keepdims=True))
        p = jnp.exp2(s - m_new)
        return m_new, jnp.exp2(m - m_new) * acc + _dot(vt, p.astype(BF16))

    m = jnp.full((1, tq), -jnp.inf, F32)
    acc = jnp.zeros((VT_ROWS, tq), F32)
    m, acc = step(kc_ref[0], vc_ref[0, 0], m, acc)
    if n_lat:
        def body(j, carry):
            j0 = pl.multiple_of(j * tk, tk)
            return step(kl_ref[0, pl.ds(j0, tk), :], vl_ref[0, 0, :, pl.ds(j0, tk)], *carry)
        m, acc = lax.fori_loop(0, n_lat // tk, body, (m, acc))
    _attn_finish(acc, o_ref)


def _attention(shift_is_safe, q, k_ctx, v_ctx, k_lat=None, v_lat=None, *, tq, tk=4096):
    b, nq, _ = q.shape
    n_ctx = k_ctx.shape[1]
    n_lat = 0 if k_lat is None else k_lat.shape[1]
    head = lambda bi, hd, i: (bi, 0, hd)
    head_t = lambda bi, hd, i: (bi, hd, 0, 0)
    in_specs = [
        pl.BlockSpec((1, tq, HEAD_PAD), lambda bi, hd, i: (bi, i, hd)),
        pl.BlockSpec((1, n_ctx, HEAD_PAD), head),
        pl.BlockSpec((1, 1, VT_ROWS, n_ctx), head_t),
    ]
    args = [q, k_ctx, v_ctx]
    if n_lat:
        in_specs += [pl.BlockSpec((1, n_lat, HEAD_PAD), head), pl.BlockSpec((1, 1, VT_ROWS, n_lat), head_t)]
        args += [k_lat, v_lat]

    def run(body, name):
        return pl.pallas_call(
            functools.partial(body, tk=tk, n_lat=n_lat),
            out_shape=jax.ShapeDtypeStruct((b, nq, MLA_WIDTH), BF16),
            grid=(b, MLA_HEADS, nq // tq),
            in_specs=in_specs,
            out_specs=pl.BlockSpec((1, tq, V_HEAD_DIM), lambda bi, hd, i: (bi, i, hd)),
            compiler_params=_params("arbitrary", "arbitrary", "arbitrary"),
            name=name,
        )

    return lax.cond(shift_is_safe,
                    lambda *a: run(_attn_shifted_kernel, "attention_shifted")(*a),
                    lambda *a: run(_attn_online_kernel, "attention_online")(*a),
                    *args)


def _pos_dft_kernel(tab_ref, uv_ref, o_ref, ue_ref, vo_ref, dp_ref, *, n):
    half = n // 2
    tr = min(ROW_SUB, half)
    tk = min(2 * ROW_SUB, half)
    inv = 1.0 / math.sqrt(n)
    u_mid = uv_ref[0, 0, half:half + 1, :].astype(F32) * inv
    anti = (lax.broadcasted_iota(jnp.int32, (tr, tr), 0) + lax.broadcasted_iota(jnp.int32, (tr, tr), 1)
            == tr).astype(BF16)
    first = lax.broadcasted_iota(jnp.int32, (tr, 1), 0) == 0

    def reverse_after(tile, head_row):
        return jnp.where(first, head_row.astype(F32), _dot(anti, tile))

    for t in range(half // tr):
        lo = slice(t * tr, (t + 1) * tr)
        hi = slice(n - (t + 1) * tr, n - t * tr)
        for c, (dst, sign) in enumerate(((ue_ref, 1.0), (vo_ref, -1.0))):
            head = uv_ref[0, c, n - t * tr:n - t * tr + 1, :] if t else jnp.zeros((1, POOL_WIDTH), BF16)
            partner = reverse_after(uv_ref[0, c, hi, :], head)
            dst[lo, :] = (uv_ref[0, c, lo, :].astype(F32) + sign * partner).astype(BF16)

    for i in range(half // tk):
        rows = slice(i * tk, (i + 1) * tk)
        k = i * tk + lax.broadcasted_iota(jnp.int32, (tk, 1), 0)
        p = _dot(tab_ref[rows, :half], ue_ref[...]) + (1 - 2 * (k % 2)).astype(F32) * u_mid
        q = _dot(tab_ref[rows, half:], vo_ref[...])
        o_ref[0, rows, :] = (p - q).astype(BF16)
        dp_ref[rows, :] = (p + q).astype(BF16)

    j = lax.broadcasted_iota(jnp.int32, (SUBLANES, half), 1)
    alt = ((1 - 2 * (j % 2)).astype(F32) * inv).astype(BF16)
    mid = (_dot(alt, ue_ref[...])[0:1] + u_mid).astype(BF16)

    for s in range(half // tr):
        head = dp_ref[half - s * tr:half - s * tr + 1, :] if s else mid
        tile = dp_ref[half - (s + 1) * tr:half - s * tr, :]
        o_ref[0, half + s * tr:half + (s + 1) * tr, :] = reverse_after(tile, head).astype(BF16)


def _pos_dft(tab, uv):
    b, _, n, _ = uv.shape
    half = n // 2
    return pl.pallas_call(
        functools.partial(_pos_dft_kernel, n=n),
        out_shape=jax.ShapeDtypeStruct((b, n, POOL_WIDTH), BF16),
        grid=(b,),
        in_specs=[_resident((half, n)),
                  pl.BlockSpec((1, 2, n, POOL_WIDTH), lambda bi: (bi, 0, 0, 0))],
        out_specs=pl.BlockSpec((1, n, POOL_WIDTH), lambda bi: (bi, 0, 0)),
        scratch_shapes=[pltpu.VMEM((half, POOL_WIDTH), BF16)] * 3,
        compiler_params=_params("arbitrary"),
        name="pos_dft",
    )(tab, uv)


def _out_proj_kernel(x_ref, mod_ref, pu_ref, prev_ref, next_ref, pg_ref, fg_ref, mg_ref, r_ref, at_ref,
                     band_ref, pw_ref, ps_ref, fw_ref, wo_ref, o_ref, ext_ref, *, tm, n):
    i = pl.program_id(1)
    ext_ref[0:POOL_HALO] = jnp.where(i > 0, prev_ref[0], 0.0).astype(BF16)
    ext_ref[POOL_HALO:POOL_HALO + tm] = pu_ref[0].astype(BF16)
    ext_ref[POOL_HALO + tm:2 * POOL_HALO + tm] = jnp.where(i < pl.num_programs(1) - 1, next_ref[0], 0.0).astype(BF16)
    ext_ref[2 * POOL_HALO + tm:] = jnp.zeros((BAND_K - ROW_SUB - 2 * POOL_HALO, POOL_WIDTH), BF16)
    gate = mod_ref[0][:, 2 * D_MODEL:]
    pair = 2 * GROUP_CH
    for r0 in range(0, tm, ROW_SUB):
        rows = slice(r0, r0 + ROW_SUB)
        t = i * tm + r0 + lax.broadcasted_iota(jnp.int32, (ROW_SUB, 1), 0)
        pooled = []
        for g, w in enumerate(POOL_WINDOWS):
            cols = slice(g * GROUP_CH, (g + 1) * GROUP_CH)
            lo = w // 2
            hi = w - lo - 1
            s = _dot(band_ref[g], ext_ref[r0:r0 + BAND_K, cols])
            cnt = (jnp.minimum(t + hi, n - 1) - jnp.maximum(t - lo, 0) + 1).astype(F32)
            pooled.append((s / cnt - pu_ref[0, rows, cols]).astype(BF16))
        parts = []
        for j in range(N_GROUPS // 2):
            cols = slice(j * pair, (j + 1) * pair)
            y = _dot(jnp.concatenate(pooled[2 * j:2 * j + 2], axis=-1), pw_ref[j]) * ps_ref[:, cols]
            parts.append((_silu(pg_ref[0, rows, cols]) * y).astype(BF16))
        for j in range(N_GROUPS // 2):
            cols = slice(j * pair, (j + 1) * pair)
            f = _dot(r_ref[0, rows, cols], fw_ref[j])
            parts.append((_silu(fg_ref[0, rows, cols]) * f).astype(BF16))
        parts.append((_silu(mg_ref[0, rows, :]) * at_ref[0, rows, :].astype(F32)).astype(BF16))
        mixed = jnp.concatenate(parts, axis=-1)
        chunk = 512
        for c0 in range(0, D_MODEL, chunk):
            y = _dot(mixed, wo_ref[:, c0:c0 + chunk])
            o_ref[0, rows, c0:c0 + chunk] = x_ref[0, rows, c0:c0 + chunk] + gate[:, c0:c0 + chunk] * y


def _out_proj(xin, mod, p, r, attn, pool_w, pool_scale, fnet_w, w_out, *, layer, per_batch_mod, tm):
    b, n, _ = xin.shape
    mod_idx = (lambda bi, i: (bi, 0, 0)) if per_batch_mod else (lambda bi, i: (0, 0, 0))
    hb = tm // POOL_HALO
    last = n // POOL_HALO - 1
    pcol = lambda width, off: pl.BlockSpec((1, tm, width), lambda bi, i: (bi, i, off // width))
    pair = 2 * GROUP_CH
    return pl.pallas_call(
        functools.partial(_out_proj_kernel, tm=tm, n=n),
        out_shape=jax.ShapeDtypeStruct((b, n, D_MODEL), F32),
        grid=(b, n // tm),
        in_specs=[
            pl.BlockSpec((1, tm, D_MODEL), lambda bi, i: (bi, i, 0)),
            pl.BlockSpec((1, 1, 3 * D_MODEL), mod_idx),
            pcol(POOL_WIDTH, COL_POOL),
            pl.BlockSpec((1, POOL_HALO, POOL_WIDTH), lambda bi, i: (bi, jnp.maximum(i * hb - 1, 0), 0)),
            pl.BlockSpec((1, POOL_HALO, POOL_WIDTH), lambda bi, i: (bi, jnp.minimum((i + 1) * hb, last), 0)),
            pcol(POOL_WIDTH, COL_POOL_GATE),
            pcol(POOL_WIDTH, COL_FNET_GATE),
            pcol(MLA_WIDTH, COL_MLA_GATE),
            pl.BlockSpec((1, tm, POOL_WIDTH), lambda bi, i: (bi, i, 0)),
            pl.BlockSpec((1, tm, MLA_WIDTH), lambda bi, i: (bi, i, 0)),
            _resident((N_GROUPS, ROW_SUB, BAND_K)),
            _resident((N_GROUPS // 2, pair, pair), layer),
            _resident((1, POOL_WIDTH)),
            _resident((N_GROUPS // 2, pair, pair), layer),
            _resident((D_MODEL, D_MODEL), layer),
        ],
        out_specs=pl.BlockSpec((1, tm, D_MODEL), lambda bi, i: (bi, i, 0)),
        scratch_shapes=[pltpu.VMEM((tm + BAND_K - ROW_SUB, POOL_WIDTH), BF16)],
        compiler_params=_params("arbitrary", "arbitrary"),
        name="out_proj",
    )(xin, mod, p, p, p, p, p, p, r, attn, _pool_band_table(), pool_w, pool_scale, fnet_w, w_out)


def _rope_tables(n):
    t = jnp.arange(n, dtype=jnp.int32)
    inv_freq = ROPE_THETA ** (-jnp.arange(N_FREQ_PER_AXIS, dtype=F32) / N_FREQ_PER_AXIS)
    ang_r = (t // GRID_W).astype(F32)[:, None] * inv_freq
    ang_c = (t % GRID_W).astype(F32)[:, None] * inv_freq
    ang = jnp.concatenate([ang_r, ang_r, ang_c, ang_c], axis=-1)
    pad = jnp.zeros((n, LANES - QK_ROPE_DIM), F32)
    return (jnp.concatenate([jnp.cos(ang), pad], axis=-1),
            jnp.concatenate([jnp.sin(ang) * _ROT_SIGN, pad], axis=-1))


def _identity_rope_tables(n):
    pad = jnp.zeros((n, LANES - QK_ROPE_DIM), F32)
    return jnp.concatenate([jnp.ones((n, QK_ROPE_DIM), F32), pad], axis=-1), jnp.zeros((n, LANES), F32)


def _channel_dft_table():
    k = np.arange(GROUP_CH)
    ang = 2.0 * np.pi * ((k[:, None] * k[None, :]) % GROUP_CH) / GROUP_CH
    tab = np.concatenate([np.cos(ang), np.sin(ang)], axis=1) / math.sqrt(GROUP_CH)
    return jnp.asarray(tab, F32).astype(BF16)


def _position_dft_table(n):
    half = n // 2
    g = 1
    while g * g < n:
        g *= 2
    period = n // g
    j = jnp.arange(half, dtype=jnp.int32)
    a = jnp.arange(half // g, dtype=jnp.int32)
    bb = jnp.arange(g, dtype=jnp.int32)
    ang_a = ((a[:, None] * j[None, :]) % period).astype(F32) * (2.0 * math.pi / period)
    ang_b = ((bb[:, None] * j[None, :]) % n).astype(F32) * (2.0 * math.pi / n)
    ca, sa = jnp.cos(ang_a)[:, None, :], jnp.sin(ang_a)[:, None, :]
    cb, sb = jnp.cos(ang_b)[None, :, :], jnp.sin(ang_b)[None, :, :]
    scale = 1.0 / math.sqrt(n)
    cos = ((ca * cb - sa * sb) * scale).reshape(half, half)
    sin = ((sa * cb + ca * sb) * scale).reshape(half, half)
    return jnp.concatenate([cos, sin], axis=1).astype(BF16)


def _pool_band_table():
    t = np.arange(ROW_SUB)[:, None]
    d = np.arange(BAND_K)[None, :] - POOL_HALO - t
    bands = [((d >= -(w // 2)) & (d <= w - w // 2 - 1)).astype(np.float32) for w in POOL_WINDOWS]
    return jnp.asarray(np.stack(bands), BF16)


def _pair_block_diag(w):
    even, odd = w[:, 0::2], w[:, 1::2]
    zero = jnp.zeros_like(even)
    return jnp.concatenate([jnp.concatenate([even, zero], axis=-1),
                            jnp.concatenate([zero, odd], axis=-1)], axis=-2).astype(BF16)


def _pack_w_in_kernel(w_ref, o_ref):
    w = w_ref[...]
    rows = w.shape[0]
    o_ref[:, :COL_MLA_GATE] = w[:, :COL_MLA_GATE].astype(BF16)
    o_ref[:, COL_MLA_GATE:COL_KROPE] = w[:, COL_MLA_GATE + QK_ROPE_DIM:].astype(BF16)
    x = w[:, COL_MLA_GATE:COL_MLA_GATE + LANES]
    lane = lax.broadcasted_iota(jnp.int32, (rows, LANES), 1)
    src = jnp.where(lane % 32 < 16, pltpu.roll(x, LANES - 16, 1), pltpu.roll(x, 16, 1))
    o_ref[:, COL_KROPE:] = jnp.where(lane < QK_ROPE_DIM, x, pltpu.roll(src, QK_ROPE_DIM, 1)).astype(BF16)


def _pack_w_in(w):
    depth, d, d_in = w.shape
    tr = ROW_SUB
    packed = pl.pallas_call(
        _pack_w_in_kernel,
        out_shape=jax.ShapeDtypeStruct((depth * d, D_IN_PACKED), BF16),
        grid=(depth * d // tr,),
        in_specs=[pl.BlockSpec((tr, d_in), lambda i: (i, 0))],
        out_specs=pl.BlockSpec((tr, D_IN_PACKED), lambda i: (i, 0)),
        compiler_params=_params("arbitrary"),
        name="pack_w_in",
    )(w.reshape(depth * d, d_in))
    return packed.reshape(depth, d, D_IN_PACKED)


def _pack_w_uq(w):
    w = w.reshape(DEPTH, LORA_RANK, MLA_HEADS, QK_HEAD_DIM)
    rope = w[..., QK_NOPE_DIM:]
    return jnp.concatenate([w[..., :QK_NOPE_DIM], rope, rope[..., _ROT_SRC]], axis=-1).reshape(
        DEPTH, LORA_RANK, MLA_HEADS * HEAD_PAD).astype(BF16)


def _pack_w_ukv(w):
    w = w.reshape(DEPTH, LORA_RANK, MLA_HEADS, QK_NOPE_DIM + V_HEAD_DIM)
    return jnp.concatenate([w[..., :QK_NOPE_DIM].reshape(DEPTH, LORA_RANK, -1),
                            w[..., QK_NOPE_DIM:].reshape(DEPTH, LORA_RANK, -1)], axis=-1).astype(BF16)


def _pack_head_w(qw, kw):
    pad = jnp.zeros((LANES - QK_ROPE_DIM,), F32)
    rows = []
    for w in (qw, kw):
        rope = w[QK_NOPE_DIM:]
        rows += [w[:QK_NOPE_DIM], jnp.concatenate([rope, pad]), jnp.concatenate([rope[_ROT_SRC], pad])]
    shift = QK_HEAD_DIM * SM_SCALE * LOG2_E * jnp.max(jnp.abs(qw)) * jnp.max(jnp.abs(kw))
    spare = (np.arange(LANES) == QK_ROPE_DIM).astype(np.float32)
    rows += [-shift * spare, jnp.asarray(spare)]
    return jnp.stack(rows), shift <= MAX_SAFE_SHIFT


def kernel(x, c, ctx, c_ctx, norm_w, w_ada, b_ada, w_in, pool_w, pool_scale, fnet_w, q_norm_w, w_uq,
           kv_norm_w, w_ukv, q_head_norm_w, k_head_norm_w, w_out):
    batch, seq, _ = x.shape
    n_ctx = ctx.shape[1]
    cc = jnp.concatenate([c, c_ctx[None, :], jnp.zeros((SUBLANES - batch - 1, D_MODEL), F32)], axis=0)
    mod_all = _ada_modulation(cc, w_ada, b_ada)

    cs_tab = _channel_dft_table()
    rope_lat = _rope_tables(seq)
    rope_ctx = _identity_rope_tables(n_ctx)
    dft_lat = _position_dft_table(seq)
    dft_ctx = _position_dft_table(n_ctx)
    tm_lat, tm_qkv, tm_ctx = 512, 512, n_ctx

    w_in_p, wuq, wukv = _pack_w_in(w_in), _pack_w_uq(w_uq), _pack_w_ukv(w_ukv)
    pw, fw, wo = _pair_block_diag(pool_w), _pair_block_diag(fnet_w), w_out.astype(BF16)

    for l in range(DEPTH):
        mod_lat = mod_all[l, :batch][:, None, :]
        mod_ctx = mod_all[l, batch:batch + 1][:, None, :]
        nw = norm_w[l][None, :]
        head_w, shift_is_safe = _pack_head_w(q_head_norm_w[l], k_head_norm_w[l])
        qnw, kvnw = q_norm_w[l][None, :], kv_norm_w[l][None, :]
        ps = pool_scale[l][None, :]
        ctx_out = l < DEPTH - 1

        if ctx_out:
            p_c, uv_c = _in_proj(ctx, mod_ctx, nw, w_in_p, cs_tab, layer=l, per_batch_mod=False, tm=tm_ctx)
            q_c, k_c, v_c = _qkv(p_c, qnw, kvnw, wuq, wukv, head_w, *rope_ctx, layer=l, tm=tm_ctx)
        else:
            p_c = _kv_in_proj(ctx, mod_ctx, nw, w_in_p, layer=l, tm=tm_ctx)
            k_c, v_c = _qkv(p_c, qnw, kvnw, wuq, wukv, head_w, *rope_ctx, layer=l, tm=tm_ctx, with_q=False)
        p_l, uv_l = _in_proj(x, mod_lat, nw, w_in_p, cs_tab, layer=l, per_batch_mod=True, tm=tm_lat)
        q_l, k_l, v_l = _qkv(p_l, qnw, kvnw, wuq, wukv, head_w, *rope_lat, layer=l, tm=tm_qkv)

        attn_l = _attention(shift_is_safe, q_l, k_c, v_c, k_l, v_l, tq=2048)
        r_l = _pos_dft(dft_lat, uv_l)
        x_new = _out_proj(x, mod_lat, p_l, r_l, attn_l, pw, ps, fw, wo, layer=l, per_batch_mod=True, tm=tm_lat)
        if ctx_out:
            attn_c = _attention(shift_is_safe, q_c, k_c, v_c, tq=n_ctx)
            r_c = _pos_dft(dft_ctx, uv_c)
            ctx = _out_proj(ctx, mod_ctx, p_c, r_c, attn_c, pw, ps, fw, wo, layer=l, per_batch_mod=False,
                            tm=tm_ctx)
        x = x_new
    return x
```

```python
import functools
import math

import numpy as np
import jax
import jax.numpy as jnp
from jax import lax
from jax.experimental import pallas as pl
from jax.experimental.pallas import tpu as pltpu

D_MODEL = 2048
DEPTH = 2
GRID_W = 64
POOL_WIDTH = 512
POOL_WINDOWS = (2, 4, 8, 16)
GROUP_CH = 128
N_GROUPS = 4
MLA_HEADS = 8
QK_NOPE_DIM = 128
QK_ROPE_DIM = 64
QK_HEAD_DIM = QK_NOPE_DIM + QK_ROPE_DIM
V_HEAD_DIM = 128
MLA_WIDTH = MLA_HEADS * V_HEAD_DIM
LORA_RANK = 512
N_FREQ_PER_AXIS = QK_ROPE_DIM // 4
ROPE_THETA = 10000.0
EPS = 1e-6
SM_SCALE = QK_HEAD_DIM ** -0.5
LOG2_E = math.log2(math.e)
MAX_SAFE_SHIFT = 60.0

COL_POOL = 0
COL_POOL_GATE = 512
COL_FNET = 1024
COL_FNET_GATE = 1536
COL_CQ = 2048
COL_CKV = 2560
COL_MLA_GATE = 3072
COL_KROPE = 4096
D_IN_PACKED = 4224
HEAD_PAD = 256

LANES = 128
SUBLANES = 8
VMEM_LIMIT = 56 * 1024 * 1024
ROW_SUB = 256
VT_ROWS = V_HEAD_DIM + 2 * SUBLANES
ATTN_CHAIN = 1024
QKV_SUB = 256
POOL_HALO = 2 * SUBLANES
BAND_K = ROW_SUB + LANES

F32 = jnp.float32
BF16 = jnp.bfloat16

_ROT_SRC = np.array([i + 16 if (i % 32) < 16 else i - 16 for i in range(QK_ROPE_DIM)])
_ROT_SIGN = np.array([-1.0 if (i % 32) < 16 else 1.0 for i in range(QK_ROPE_DIM)], np.float32)


def _silu(x):
    return x * jax.nn.sigmoid(x)


def _dot(a, b):
    return jnp.dot(a, b, preferred_element_type=F32)


def _params(*sem):
    return pltpu.CompilerParams(dimension_semantics=sem, vmem_limit_bytes=VMEM_LIMIT)


def _resident(shape, layer=None):
    nd = len(shape)
    if layer is None:
        return pl.BlockSpec(shape, lambda *_: (0,) * nd, pipeline_mode=pl.Buffered(1))
    return pl.BlockSpec((None,) + tuple(shape), lambda *_: (layer,) + (0,) * nd, pipeline_mode=pl.Buffered(1))


def _ada_kernel(c_ref, w_ref, b_ref, o_ref):
    s = _silu(c_ref[...]).astype(BF16)
    o_ref[0] = _dot(s, w_ref[0].astype(BF16)) + b_ref[0]


def _ada_modulation(cc, w_ada, b_ada):
    tn = 512
    n3 = 3 * D_MODEL
    return pl.pallas_call(
        _ada_kernel,
        out_shape=jax.ShapeDtypeStruct((DEPTH, SUBLANES, n3), F32),
        grid=(DEPTH, n3 // tn),
        in_specs=[
            pl.BlockSpec((SUBLANES, D_MODEL), lambda l, j: (0, 0)),
            pl.BlockSpec((1, D_MODEL, tn), lambda l, j: (l, 0, j)),
            pl.BlockSpec((1, 1, tn), lambda l, j: (l, 0, j)),
        ],
        out_specs=pl.BlockSpec((1, SUBLANES, tn), lambda l, j: (l, 0, j)),
        compiler_params=_params("arbitrary", "arbitrary"),
        name="ada_modulation",
    )(cc, w_ada, b_ada.reshape(DEPTH, 1, n3))


def _kv_in_proj_kernel(x_ref, mod_ref, nw_ref, w_ref, p_ref):
    mod = mod_ref[0]
    x = x_ref[0]
    r = lax.rsqrt(jnp.mean(x * x, axis=-1, keepdims=True) + EPS)
    h = (x * r * ((1.0 + mod[:, D_MODEL:2 * D_MODEL]) * nw_ref[...]) + mod[:, :D_MODEL]).astype(BF16)
    p_ref[0, :, :LORA_RANK] = _dot(h, w_ref[:, COL_CKV:COL_CKV + LORA_RANK])
    p_ref[0, :, LORA_RANK:] = _dot(h, w_ref[:, COL_KROPE:])


def _in_proj_kernel(x_ref, mod_ref, nw_ref, w_ref, cs_ref, p_ref, uv_ref):
    mod = mod_ref[0]
    shift = mod[:, :D_MODEL]
    scale1 = (1.0 + mod[:, D_MODEL:2 * D_MODEL]) * nw_ref[...]
    tm = x_ref.shape[1]
    for r0 in range(0, tm, ROW_SUB):
        rows = slice(r0, min(r0 + ROW_SUB, tm))
        x = x_ref[0, rows, :]
        r = lax.rsqrt(jnp.mean(x * x, axis=-1, keepdims=True) + EPS)
        h = (x * r * scale1 + shift).astype(BF16)
        chunk = 512
        for c0 in range(0, D_IN_PACKED, chunk):
            c1 = min(c0 + chunk, D_IN_PACKED)
            pc = _dot(h, w_ref[:, c0:c1])
            p_ref[0, rows, c0:c1] = pc
            if c0 == COL_FNET:
                g = pc.astype(BF16)
                for hd in range(N_GROUPS):
                    cols = slice(hd * GROUP_CH, (hd + 1) * GROUP_CH)
                    uv = _dot(g[:, cols], cs_ref[...])
                    uv_ref[0, 0, rows, cols] = uv[:, :GROUP_CH].astype(BF16)
                    uv_ref[0, 1, rows, cols] = uv[:, GROUP_CH:].astype(BF16)


def _kv_in_proj(xin, mod, norm_w, w_packed, *, layer, tm):
    b, n, _ = xin.shape
    width = LORA_RANK + LANES
    return pl.pallas_call(
        _kv_in_proj_kernel,
        out_shape=jax.ShapeDtypeStruct((b, n, width), F32),
        grid=(b, n // tm),
        in_specs=[
            pl.BlockSpec((1, tm, D_MODEL), lambda bi, i: (bi, i, 0)),
            pl.BlockSpec((1, 1, 3 * D_MODEL), lambda bi, i: (0, 0, 0)),
            _resident((1, D_MODEL)),
            _resident((D_MODEL, D_IN_PACKED), layer),
        ],
        out_specs=pl.BlockSpec((1, tm, width), lambda bi, i: (bi, i, 0)),
        compiler_params=_params("arbitrary", "arbitrary"),
        name="kv_in_proj",
    )(xin, mod, norm_w, w_packed)


def _in_proj(xin, mod, norm_w, w_packed, cs_tab, *, layer, per_batch_mod, tm):
    b, n, _ = xin.shape
    mod_idx = (lambda bi, i: (bi, 0, 0)) if per_batch_mod else (lambda bi, i: (0, 0, 0))
    return pl.pallas_call(
        _in_proj_kernel,
        out_shape=(jax.ShapeDtypeStruct((b, n, D_IN_PACKED), F32),
                   jax.ShapeDtypeStruct((b, 2, n, POOL_WIDTH), BF16)),
        grid=(b, n // tm),
        in_specs=[
            pl.BlockSpec((1, tm, D_MODEL), lambda bi, i: (bi, i, 0)),
            pl.BlockSpec((1, 1, 3 * D_MODEL), mod_idx),
            _resident((1, D_MODEL)),
            _resident((D_MODEL, D_IN_PACKED), layer),
            _resident((GROUP_CH, 2 * GROUP_CH)),
        ],
        out_specs=(pl.BlockSpec((1, tm, D_IN_PACKED), lambda bi, i: (bi, i, 0)),
                   pl.BlockSpec((1, 2, tm, POOL_WIDTH), lambda bi, i: (bi, 0, i, 0))),
        compiler_params=_params("arbitrary", "arbitrary"),
        name="in_proj",
    )(xin, mod, norm_w, w_packed, cs_tab)


def _rms(x, w):
    return x * lax.rsqrt(jnp.mean(x * x, axis=-1, keepdims=True) + EPS) * w


def _rope(slab, t_cos, t_sin):
    return slab * t_cos + pltpu.roll(slab, LANES // 2, 1) * t_sin


def _normed_q_head(qh, cos, sin, hw):
    q_nope_w, q_rope_w, q_rot_w, q_pad = hw[0:1], hw[1:2], hw[2:3], hw[6:7]
    qn, qs = qh[:, :LANES], qh[:, LANES:]
    ss = jnp.sum(qn * qn + 0.5 * (qs * qs), axis=-1, keepdims=True)
    r = lax.rsqrt(ss * (1.0 / QK_HEAD_DIM) + EPS) * (SM_SCALE * LOG2_E)
    rope = _rope(qs, cos * q_rope_w, sin * q_rot_w) * r + q_pad
    return jnp.concatenate([qn * r * q_nope_w, rope], axis=-1).astype(BF16)


def _qkv_kernel(*refs, with_q):
    if with_q:
        (cq_ref, ckv_ref, kr_ref, qnw_ref, kvnw_ref, wuq_ref, wukv_ref, hw_ref, cos_ref, sin_ref,
         q_ref, k_ref, v_ref) = refs
    else:
        ckv_ref, kr_ref, kvnw_ref, wukv_ref, hw_ref, cos_ref, sin_ref, k_ref, v_ref = refs
    hw = hw_ref[...]
    k_nope_w, k_rope_w, k_rot_w, k_pad = hw[3:4], hw[4:5], hw[5:6], hw[7:8]
    tm = ckv_ref.shape[1]
    sub = min(QKV_SUB, tm)
    one_row = (lax.broadcasted_iota(jnp.int32, (VT_ROWS - V_HEAD_DIM, sub), 0) == 0).astype(BF16)
    inv_dim = 1.0 / QK_HEAD_DIM

    for r0 in range(0, tm, sub):
        rows = slice(r0, r0 + sub)
        cos = cos_ref[rows, :]
        sin = sin_ref[rows, :]
        if with_q:
            cqn = _rms(cq_ref[0, rows, :], qnw_ref[...]).astype(BF16)
            for hd in range(MLA_HEADS):
                cols = slice(hd * HEAD_PAD, (hd + 1) * HEAD_PAD)
                q_ref[0, rows, cols] = _normed_q_head(_dot(cqn, wuq_ref[:, cols]), cos, sin, hw)

        ckvn = _rms(ckv_ref[0, rows, :], kvnw_ref[...]).astype(BF16)
        ks = kr_ref[0, rows, :]
        ss_rope = 0.5 * jnp.sum(ks * ks, axis=-1, keepdims=True)
        k_rope = _rope(ks, cos * k_rope_w, sin * k_rot_w)
        for hd in range(MLA_HEADS):
            kn = _dot(ckvn, wukv_ref[:, hd * LANES:(hd + 1) * LANES])
            ss = jnp.sum(kn * kn, axis=-1, keepdims=True) + ss_rope
            r = lax.rsqrt(ss * inv_dim + EPS)
            k_ref[0, rows, hd * HEAD_PAD:(hd + 1) * HEAD_PAD] = jnp.concatenate(
                [kn * r * k_nope_w, k_rope * r + k_pad], axis=-1).astype(BF16)
        v = _dot(ckvn, wukv_ref[:, MLA_HEADS * LANES:])
        for hd in range(MLA_HEADS):
            v_ref[0, hd, 0:V_HEAD_DIM, rows] = v[:, hd * LANES:(hd + 1) * LANES].T.astype(BF16)
            v_ref[0, hd, V_HEAD_DIM:, rows] = one_row


def _qkv(p, q_norm_w, kv_norm_w, wuq, wukv, head_w, cos_tab, sin_tab, *, layer, tm, with_q=True):
    b, n, _ = p.shape
    row = lambda bi, i: (i, 0)
    ckv_col, kr_col = (COL_CKV, COL_KROPE) if with_q else (0, LORA_RANK)
    qk_shape = jax.ShapeDtypeStruct((b, n, MLA_HEADS * HEAD_PAD), BF16)
    qk_spec = pl.BlockSpec((1, tm, MLA_HEADS * HEAD_PAD), lambda bi, i: (bi, i, 0))
    ckv_spec = pl.BlockSpec((1, tm, LORA_RANK), lambda bi, i: (bi, i, ckv_col // LORA_RANK))
    kr_spec = pl.BlockSpec((1, tm, LANES), lambda bi, i: (bi, i, kr_col // LANES))
    kv_w_specs = [_resident((1, LORA_RANK)), _resident((LORA_RANK, 2 * MLA_WIDTH), layer)]
    tail_specs = [_resident((SUBLANES, LANES)), pl.BlockSpec((tm, LANES), row), pl.BlockSpec((tm, LANES), row)]
    kv_out = ((qk_shape, jax.ShapeDtypeStruct((b, MLA_HEADS, VT_ROWS, n), BF16)),
              (qk_spec, pl.BlockSpec((1, MLA_HEADS, VT_ROWS, tm), lambda bi, i: (bi, 0, 0, i))))
    if with_q:
        in_specs = [pl.BlockSpec((1, tm, LORA_RANK), lambda bi, i: (bi, i, COL_CQ // LORA_RANK)), ckv_spec, kr_spec,
                    _resident((1, LORA_RANK)), kv_w_specs[0],
                    _resident((LORA_RANK, MLA_HEADS * HEAD_PAD), layer), kv_w_specs[1]] + tail_specs
        args = (p, p, p, q_norm_w, kv_norm_w, wuq, wukv, head_w, cos_tab, sin_tab)
        out_shape, out_specs = (qk_shape,) + kv_out[0], (qk_spec,) + kv_out[1]
    else:
        in_specs = [ckv_spec, kr_spec] + kv_w_specs + tail_specs
        args = (p, p, kv_norm_w, wukv, head_w, cos_tab, sin_tab)
        out_shape, out_specs = kv_out
    return pl.pallas_call(
        functools.partial(_qkv_kernel, with_q=with_q),
        out_shape=out_shape,
        grid=(b, n // tm),
        in_specs=in_specs,
        out_specs=out_specs,
        compiler_params=_params("arbitrary", "arbitrary"),
        name="qkv" if with_q else "kv",
    )(*args)


def _scores_t(k, q):
    return lax.dot_general(k, q, (((1,), (1,)), ((), ())), preferred_element_type=F32)


def _attn_finish(acc, o_ref, rows=slice(None)):
    o_ref[0, rows, :] = (acc[:V_HEAD_DIM] / acc[V_HEAD_DIM:V_HEAD_DIM + 1]).T.astype(BF16)


def _attn_shifted_kernel(*refs, tk, n_lat):
    if n_lat:
        q_ref, kc_ref, vc_ref, kl_ref, vl_ref, o_ref = refs
    else:
        q_ref, kc_ref, vc_ref, o_ref = refs
    tq = q_ref.shape[1]
    cw = min(tq, ATTN_CHAIN)
    chains = [slice(r0, r0 + cw) for r0 in range(0, tq, cw)]
    qs = [q_ref[0, rows, :] for rows in chains]

    def tile(q, k, vt):
        return _dot(vt, jnp.exp2(_scores_t(k, q)).astype(BF16))

    accs = [tile(q, kc_ref[0], vc_ref[0, 0]) for q in qs]
    for j in range(n_lat // tk):
        k, vt = kl_ref[0, j * tk:(j + 1) * tk, :], vl_ref[0, 0, :, j * tk:(j + 1) * tk]
        accs = [acc + tile(q, k, vt) for q, acc in zip(qs, accs)]
    for rows, acc in zip(chains, accs):
        _attn_finish(acc, o_ref, rows)


def _attn_ctx_heads_kernel(q_ref, k_ref, v_ref, o_ref):
    for hd in range(MLA_HEADS):
        cols = slice(hd * HEAD_PAD, (hd + 1) * HEAD_PAD)
        acc = _dot(v_ref[0, hd], jnp.exp2(_scores_t(k_ref[0, :, cols], q_ref[0, :, cols])).astype(BF16))
        o_ref[0, :, hd * V_HEAD_DIM:(hd + 1) * V_HEAD_DIM] = (
            acc[:V_HEAD_DIM] / acc[V_HEAD_DIM:V_HEAD_DIM + 1]).T.astype(BF16)


def _attn_online_kernel(*refs, tk, n_lat):
    if n_lat:
        q_ref, kc_ref, vc_ref, kl_ref, vl_ref, o_ref = refs
    else:
        q_ref, kc_ref, vc_ref, o_ref = refs
    q = q_ref[0]
    tq = q.shape[0]

    def step(k, vt, m, acc):
        s = _scores_t(k, q)
        m_new = jnp.maximum(m, jnp.max(s, axis=0, keepdims=True))
        p = jnp.exp2(s - m_new)
        return m_new, jnp.exp2(m - m_new) * acc + _dot(vt, p.astype(BF16))

    m = jnp.full((1, tq), -jnp.inf, F32)
    acc = jnp.zeros((VT_ROWS, tq), F32)
    m, acc = step(kc_ref[0], vc_ref[0, 0], m, acc)
    if n_lat:
        def body(j, carry):
            j0 = pl.multiple_of(j * tk, tk)
            return step(kl_ref[0, pl.ds(j0, tk), :], vl_ref[0, 0, :, pl.ds(j0, tk)], *carry)
        m, acc = lax.fori_loop(0, n_lat // tk, body, (m, acc))
    _attn_finish(acc, o_ref)


def _attention(shift_is_safe, q, k_ctx, v_ctx, k_lat=None, v_lat=None, *, tq, tk=4096):
    b, nq, _ = q.shape
    n_ctx = k_ctx.shape[1]
    n_lat = 0 if k_lat is None else k_lat.shape[1]
    head = lambda bi, hd, i: (bi, 0, hd)
    head_t = lambda bi, hd, i: (bi, hd, 0, 0)
    in_specs = [
        pl.BlockSpec((1, tq, HEAD_PAD), lambda bi, hd, i: (bi, i, hd)),
        pl.BlockSpec((1, n_ctx, HEAD_PAD), head),
        pl.BlockSpec((1, 1, VT_ROWS, n_ctx), head_t),
    ]
    args = [q, k_ctx, v_ctx]
    if n_lat:
        in_specs += [pl.BlockSpec((1, n_lat, HEAD_PAD), head), pl.BlockSpec((1, 1, VT_ROWS, n_lat), head_t)]
        args += [k_lat, v_lat]

    def run(body, name):
        return pl.pallas_call(
            functools.partial(body, tk=tk, n_lat=n_lat),
            out_shape=jax.ShapeDtypeStruct((b, nq, MLA_WIDTH), BF16),
            grid=(b, MLA_HEADS, nq // tq),
            in_specs=in_specs,
            out_specs=pl.BlockSpec((1, tq, V_HEAD_DIM), lambda bi, hd, i: (bi, i, hd)),
            compiler_params=_params("arbitrary", "arbitrary", "arbitrary"),
            name=name,
        )

    def run_ctx_heads(qc, kc, vc):
        return pl.pallas_call(
            _attn_ctx_heads_kernel,
            out_shape=jax.ShapeDtypeStruct((b, nq, MLA_WIDTH), BF16),
            grid=(b,),
            in_specs=[pl.BlockSpec((1, nq, MLA_HEADS * HEAD_PAD), lambda bi: (bi, 0, 0)),
                      pl.BlockSpec((1, n_ctx, MLA_HEADS * HEAD_PAD), lambda bi: (bi, 0, 0)),
                      pl.BlockSpec((1, MLA_HEADS, VT_ROWS, n_ctx), lambda bi: (bi, 0, 0, 0))],
            out_specs=pl.BlockSpec((1, nq, MLA_WIDTH), lambda bi: (bi, 0, 0)),
            compiler_params=_params("arbitrary"),
            name="attention_ctx_heads",
        )(qc, kc, vc)

    shifted = (lambda *a: run(_attn_shifted_kernel, "attention_shifted")(*a)) if n_lat else run_ctx_heads
    return lax.cond(shift_is_safe,
                    shifted,
                    lambda *a: run(_attn_online_kernel, "attention_online")(*a),
                    *args)


def _pos_dft_kernel(tab_ref, uv_ref, o_ref, ue_ref, vo_ref, dp_ref, *, n):
    half = n // 2
    tr = min(ROW_SUB, half)
    tk = min(2 * ROW_SUB, half)
    inv = 1.0 / math.sqrt(n)
    u_mid = uv_ref[0, 0, half:half + 1, :].astype(F32) * inv
    anti = (lax.broadcasted_iota(jnp.int32, (tr, tr), 0) + lax.broadcasted_iota(jnp.int32, (tr, tr), 1)
            == tr).astype(BF16)
    first = lax.broadcasted_iota(jnp.int32, (tr, 1), 0) == 0

    def reverse_after(tile, head_row):
        return jnp.where(first, head_row.astype(F32), _dot(anti, tile))

    for t in range(half // tr):
        lo = slice(t * tr, (t + 1) * tr)
        hi = slice(n - (t + 1) * tr, n - t * tr)
        for c, (dst, sign) in enumerate(((ue_ref, 1.0), (vo_ref, -1.0))):
            head = uv_ref[0, c, n - t * tr:n - t * tr + 1, :] if t else jnp.zeros((1, POOL_WIDTH), BF16)
            partner = reverse_after(uv_ref[0, c, hi, :], head)
            dst[lo, :] = (uv_ref[0, c, lo, :].astype(F32) + sign * partner).astype(BF16)

    for i in range(half // tk):
        rows = slice(i * tk, (i + 1) * tk)
        k = i * tk + lax.broadcasted_iota(jnp.int32, (tk, 1), 0)
        p = _dot(tab_ref[rows, :half], ue_ref[...]) + (1 - 2 * (k % 2)).astype(F32) * u_mid
        q = _dot(tab_ref[rows, half:], vo_ref[...])
        o_ref[0, rows, :] = (p - q).astype(BF16)
        dp_ref[rows, :] = (p + q).astype(BF16)

    j = lax.broadcasted_iota(jnp.int32, (SUBLANES, half), 1)
    alt = ((1 - 2 * (j % 2)).astype(F32) * inv).astype(BF16)
    mid = (_dot(alt, ue_ref[...])[0:1] + u_mid).astype(BF16)

    for s in range(half // tr):
        head = dp_ref[half - s * tr:half - s * tr + 1, :] if s else mid
        tile = dp_ref[half - (s + 1) * tr:half - s * tr, :]
        o_ref[0, half + s * tr:half + (s + 1) * tr, :] = reverse_after(tile, head).astype(BF16)


def _pos_dft(tab, uv):
    b, _, n, _ = uv.shape
    half = n // 2
    return pl.pallas_call(
        functools.partial(_pos_dft_kernel, n=n),
        out_shape=jax.ShapeDtypeStruct((b, n, POOL_WIDTH), BF16),
        grid=(b,),
        in_specs=[_resident((half, n)),
                  pl.BlockSpec((1, 2, n, POOL_WIDTH), lambda bi: (bi, 0, 0, 0))],
        out_specs=pl.BlockSpec((1, n, POOL_WIDTH), lambda bi: (bi, 0, 0)),
        scratch_shapes=[pltpu.VMEM((half, POOL_WIDTH), BF16)] * 3,
        compiler_params=_params("arbitrary"),
        name="pos_dft",
    )(tab, uv)


def _out_proj_kernel(x_ref, mod_ref, pu_ref, prev_ref, next_ref, pg_ref, fg_ref, mg_ref, r_ref, at_ref,
                     band_ref, pw_ref, ps_ref, fw_ref, wo_ref, o_ref, ext_ref, *, tm, n):
    i = pl.program_id(1)
    ext_ref[0:POOL_HALO] = jnp.where(i > 0, prev_ref[0], 0.0).astype(BF16)
    ext_ref[POOL_HALO:POOL_HALO + tm] = pu_ref[0].astype(BF16)
    ext_ref[POOL_HALO + tm:2 * POOL_HALO + tm] = jnp.where(i < pl.num_programs(1) - 1, next_ref[0], 0.0).astype(BF16)
    ext_ref[2 * POOL_HALO + tm:] = jnp.zeros((BAND_K - ROW_SUB - 2 * POOL_HALO, POOL_WIDTH), BF16)
    gate = mod_ref[0][:, 2 * D_MODEL:]
    pair = 2 * GROUP_CH
    for r0 in range(0, tm, ROW_SUB):
        rows = slice(r0, r0 + ROW_SUB)
        t = i * tm + r0 + lax.broadcasted_iota(jnp.int32, (ROW_SUB, 1), 0)
        pooled = []
        for g, w in enumerate(POOL_WINDOWS):
            cols = slice(g * GROUP_CH, (g + 1) * GROUP_CH)
            lo = w // 2
            hi = w - lo - 1
            s = _dot(band_ref[g], ext_ref[r0:r0 + BAND_K, cols])
            cnt = (jnp.minimum(t + hi, n - 1) - jnp.maximum(t - lo, 0) + 1).astype(F32)
            pooled.append((s / cnt - pu_ref[0, rows, cols]).astype(BF16))
        parts = []
        for j in range(N_GROUPS // 2):
            cols = slice(j * pair, (j + 1) * pair)
            y = _dot(jnp.concatenate(pooled[2 * j:2 * j + 2], axis=-1), pw_ref[j]) * ps_ref[:, cols]
            parts.append((_silu(pg_ref[0, rows, cols]) * y).astype(BF16))
        for j in range(N_GROUPS // 2):
            cols = slice(j * pair, (j + 1) * pair)
            f = _dot(r_ref[0, rows, cols], fw_ref[j])
            parts.append((_silu(fg_ref[0, rows, cols]) * f).astype(BF16))
        parts.append((_silu(mg_ref[0, rows, :]) * at_ref[0, rows, :].astype(F32)).astype(BF16))
        mixed = jnp.concatenate(parts, axis=-1)
        chunk = 512
        for c0 in range(0, D_MODEL, chunk):
            y = _dot(mixed, wo_ref[:, c0:c0 + chunk])
            o_ref[0, rows, c0:c0 + chunk] = x_ref[0, rows, c0:c0 + chunk] + gate[:, c0:c0 + chunk] * y


def _out_proj(xin, mod, p, r, attn, pool_w, pool_scale, fnet_w, w_out, *, layer, per_batch_mod, tm):
    b, n, _ = xin.shape
    mod_idx = (lambda bi, i: (bi, 0, 0)) if per_batch_mod else (lambda bi, i: (0, 0, 0))
    hb = tm // POOL_HALO
    last = n // POOL_HALO - 1
    pcol = lambda width, off: pl.BlockSpec((1, tm, width), lambda bi, i: (bi, i, off // width))
    pair = 2 * GROUP_CH
    return pl.pallas_call(
        functools.partial(_out_proj_kernel, tm=tm, n=n),
        out_shape=jax.ShapeDtypeStruct((b, n, D_MODEL), F32),
        grid=(b, n // tm),
        in_specs=[
            pl.BlockSpec((1, tm, D_MODEL), lambda bi, i: (bi, i, 0)),
            pl.BlockSpec((1, 1, 3 * D_MODEL), mod_idx),
            pcol(POOL_WIDTH, COL_POOL),
            pl.BlockSpec((1, POOL_HALO, POOL_WIDTH), lambda bi, i: (bi, jnp.maximum(i * hb - 1, 0), 0)),
            pl.BlockSpec((1, POOL_HALO, POOL_WIDTH), lambda bi, i: (bi, jnp.minimum((i + 1) * hb, last), 0)),
            pcol(POOL_WIDTH, COL_POOL_GATE),
            pcol(POOL_WIDTH, COL_FNET_GATE),
            pcol(MLA_WIDTH, COL_MLA_GATE),
            pl.BlockSpec((1, tm, POOL_WIDTH), lambda bi, i: (bi, i, 0)),
            pl.BlockSpec((1, tm, MLA_WIDTH), lambda bi, i: (bi, i, 0)),
            _resident((N_GROUPS, ROW_SUB, BAND_K)),
            _resident((N_GROUPS // 2, pair, pair), layer),
            _resident((1, POOL_WIDTH)),
            _resident((N_GROUPS // 2, pair, pair), layer),
            _resident((D_MODEL, D_MODEL), layer),
        ],
        out_specs=pl.BlockSpec((1, tm, D_MODEL), lambda bi, i: (bi, i, 0)),
        scratch_shapes=[pltpu.VMEM((tm + BAND_K - ROW_SUB, POOL_WIDTH), BF16)],
        compiler_params=_params("arbitrary", "arbitrary"),
        name="out_proj",
    )(xin, mod, p, p, p, p, p, p, r, attn, _pool_band_table(), pool_w, pool_scale, fnet_w, w_out)


def _rope_tables(n):
    t = jnp.arange(n, dtype=jnp.int32)
    inv_freq = ROPE_THETA ** (-jnp.arange(N_FREQ_PER_AXIS, dtype=F32) / N_FREQ_PER_AXIS)
    ang_r = (t // GRID_W).astype(F32)[:, None] * inv_freq
    ang_c = (t % GRID_W).astype(F32)[:, None] * inv_freq
    ang = jnp.concatenate([ang_r, ang_r, ang_c, ang_c], axis=-1)
    pad = jnp.zeros((n, LANES - QK_ROPE_DIM), F32)
    return (jnp.concatenate([jnp.cos(ang), pad], axis=-1),
            jnp.concatenate([jnp.sin(ang) * _ROT_SIGN, pad], axis=-1))


def _identity_rope_tables(n):
    pad = jnp.zeros((n, LANES - QK_ROPE_DIM), F32)
    return jnp.concatenate([jnp.ones((n, QK_ROPE_DIM), F32), pad], axis=-1), jnp.zeros((n, LANES), F32)


def _channel_dft_table():
    k = np.arange(GROUP_CH)
    ang = 2.0 * np.pi * ((k[:, None] * k[None, :]) % GROUP_CH) / GROUP_CH
    tab = np.concatenate([np.cos(ang), np.sin(ang)], axis=1) / math.sqrt(GROUP_CH)
    return jnp.asarray(tab, F32).astype(BF16)


def _position_dft_table(n):
    half = n // 2
    g = 1
    while g * g < n:
        g *= 2
    period = n // g
    j = jnp.arange(half, dtype=jnp.int32)
    a = jnp.arange(half // g, dtype=jnp.int32)
    bb = jnp.arange(g, dtype=jnp.int32)
    ang_a = ((a[:, None] * j[None, :]) % period).astype(F32) * (2.0 * math.pi / period)
    ang_b = ((bb[:, None] * j[None, :]) % n).astype(F32) * (2.0 * math.pi / n)
    ca, sa = jnp.cos(ang_a)[:, None, :], jnp.sin(ang_a)[:, None, :]
    cb, sb = jnp.cos(ang_b)[None, :, :], jnp.sin(ang_b)[None, :, :]
    scale = 1.0 / math.sqrt(n)
    cos = ((ca * cb - sa * sb) * scale).reshape(half, half)
    sin = ((sa * cb + ca * sb) * scale).reshape(half, half)
    return jnp.concatenate([cos, sin], axis=1).astype(BF16)


def _pool_band_table():
    t = np.arange(ROW_SUB)[:, None]
    d = np.arange(BAND_K)[None, :] - POOL_HALO - t
    bands = [((d >= -(w // 2)) & (d <= w - w // 2 - 1)).astype(np.float32) for w in POOL_WINDOWS]
    return jnp.asarray(np.stack(bands), BF16)


def _pair_block_diag(w):
    even, odd = w[:, 0::2], w[:, 1::2]
    zero = jnp.zeros_like(even)
    return jnp.concatenate([jnp.concatenate([even, zero], axis=-1),
                            jnp.concatenate([zero, odd], axis=-1)], axis=-2).astype(BF16)


def _pack_w_in_kernel(w_ref, o_ref):
    w = w_ref[...]
    rows = w.shape[0]
    o_ref[:, :COL_MLA_GATE] = w[:, :COL_MLA_GATE].astype(BF16)
    o_ref[:, COL_MLA_GATE:COL_KROPE] = w[:, COL_MLA_GATE + QK_ROPE_DIM:].astype(BF16)
    x = w[:, COL_MLA_GATE:COL_MLA_GATE + LANES]
    lane = lax.broadcasted_iota(jnp.int32, (rows, LANES), 1)
    src = jnp.where(lane % 32 < 16, pltpu.roll(x, LANES - 16, 1), pltpu.roll(x, 16, 1))
    o_ref[:, COL_KROPE:] = jnp.where(lane < QK_ROPE_DIM, x, pltpu.roll(src, QK_ROPE_DIM, 1)).astype(BF16)


def _pack_w_in(w):
    depth, d, d_in = w.shape
    tr = ROW_SUB
    packed = pl.pallas_call(
        _pack_w_in_kernel,
        out_shape=jax.ShapeDtypeStruct((depth * d, D_IN_PACKED), BF16),
        grid=(depth * d // tr,),
        in_specs=[pl.BlockSpec((tr, d_in), lambda i: (i, 0))],
        out_specs=pl.BlockSpec((tr, D_IN_PACKED), lambda i: (i, 0)),
        compiler_params=_params("arbitrary"),
        name="pack_w_in",
    )(w.reshape(depth * d, d_in))
    return packed.reshape(depth, d, D_IN_PACKED)


def _pack_w_uq(w):
    w = w.reshape(DEPTH, LORA_RANK, MLA_HEADS, QK_HEAD_DIM)
    rope = w[..., QK_NOPE_DIM:]
    return jnp.concatenate([w[..., :QK_NOPE_DIM], rope, rope[..., _ROT_SRC]], axis=-1).reshape(
        DEPTH, LORA_RANK, MLA_HEADS * HEAD_PAD).astype(BF16)


def _pack_w_ukv(w):
    w = w.reshape(DEPTH, LORA_RANK, MLA_HEADS, QK_NOPE_DIM + V_HEAD_DIM)
    return jnp.concatenate([w[..., :QK_NOPE_DIM].reshape(DEPTH, LORA_RANK, -1),
                            w[..., QK_NOPE_DIM:].reshape(DEPTH, LORA_RANK, -1)], axis=-1).astype(BF16)


def _pack_head_w(qw, kw):
    pad = jnp.zeros((LANES - QK_ROPE_DIM,), F32)
    rows = []
    for w in (qw, kw):
        rope = w[QK_NOPE_DIM:]
        rows += [w[:QK_NOPE_DIM], jnp.concatenate([rope, pad]), jnp.concatenate([rope[_ROT_SRC], pad])]
    shift = QK_HEAD_DIM * SM_SCALE * LOG2_E * jnp.max(jnp.abs(qw)) * jnp.max(jnp.abs(kw))
    spare = (np.arange(LANES) == QK_ROPE_DIM).astype(np.float32)
    rows += [-shift * spare, jnp.asarray(spare)]
    return jnp.stack(rows), shift <= MAX_SAFE_SHIFT


def kernel(x, c, ctx, c_ctx, norm_w, w_ada, b_ada, w_in, pool_w, pool_scale, fnet_w, q_norm_w, w_uq,
           kv_norm_w, w_ukv, q_head_norm_w, k_head_norm_w, w_out):
    batch, seq, _ = x.shape
    n_ctx = ctx.shape[1]
    cc = jnp.concatenate([c, c_ctx[None, :], jnp.zeros((SUBLANES - batch - 1, D_MODEL), F32)], axis=0)
    mod_all = _ada_modulation(cc, w_ada, b_ada)

    cs_tab = _channel_dft_table()
    rope_lat = _rope_tables(seq)
    rope_ctx = _identity_rope_tables(n_ctx)
    dft_lat = _position_dft_table(seq)
    dft_ctx = _position_dft_table(n_ctx)
    tm_lat, tm_qkv, tm_ctx = 512, 512, n_ctx

    w_in_p, wuq, wukv = _pack_w_in(w_in), _pack_w_uq(w_uq), _pack_w_ukv(w_ukv)
    pw, fw, wo = _pair_block_diag(pool_w), _pair_block_diag(fnet_w), w_out.astype(BF16)

    for l in range(DEPTH):
        mod_lat = mod_all[l, :batch][:, None, :]
        mod_ctx = mod_all[l, batch:batch + 1][:, None, :]
        nw = norm_w[l][None, :]
        head_w, shift_is_safe = _pack_head_w(q_head_norm_w[l], k_head_norm_w[l])
        qnw, kvnw = q_norm_w[l][None, :], kv_norm_w[l][None, :]
        ps = pool_scale[l][None, :]
        ctx_out = l < DEPTH - 1

        if ctx_out:
            p_c, uv_c = _in_proj(ctx, mod_ctx, nw, w_in_p, cs_tab, layer=l, per_batch_mod=False, tm=tm_ctx)
            q_c, k_c, v_c = _qkv(p_c, qnw, kvnw, wuq, wukv, head_w, *rope_ctx, layer=l, tm=tm_ctx)
        else:
            p_c = _kv_in_proj(ctx, mod_ctx, nw, w_in_p, layer=l, tm=tm_ctx)
            k_c, v_c = _qkv(p_c, qnw, kvnw, wuq, wukv, head_w, *rope_ctx, layer=l, tm=tm_ctx, with_q=False)
        p_l, uv_l = _in_proj(x, mod_lat, nw, w_in_p, cs_tab, layer=l, per_batch_mod=True, tm=tm_lat)
        q_l, k_l, v_l = _qkv(p_l, qnw, kvnw, wuq, wukv, head_w, *rope_lat, layer=l, tm=tm_qkv)

        attn_l = _attention(shift_is_safe, q_l, k_c, v_c, k_l, v_l, tq=2048)
        r_l = _pos_dft(dft_lat, uv_l)
        x_new = _out_proj(x, mod_lat, p_l, r_l, attn_l, pw, ps, fw, wo, layer=l, per_batch_mod=True, tm=tm_lat)
        if ctx_out:
            attn_c = _attention(shift_is_safe, q_c, k_c, v_c, tq=n_ctx)
            r_c = _pos_dft(dft_ctx, uv_c)
            ctx = _out_proj(ctx, mod_ctx, p_c, r_c, attn_c, pw, ps, fw, wo, layer=l, per_batch_mod=False,
                            tm=tm_ctx)
        x = x_new
    return x
```
